```python
import jax, jax.numpy as jnp
from jax import lax
import numpy as np

D_MODEL = 2048
BATCH = 8
SEQ = 2048
DEPTH = 2

N_MIXERS = 2
HEAD_DIM = 128
HEADS_PER_GROUP = 4
DILATED_PATTERNS = ((128, 1), (512, 4), (2048, 16))
N_ATTN_GROUPS = len(DILATED_PATTERNS)
N_HEADS = HEADS_PER_GROUP * N_ATTN_GROUPS
ATTN_WIDTH = N_HEADS * HEAD_DIM
ROT_DIM = HEAD_DIM // 4
ROPE_THETA = 500000.0
ATTN_BLOCK = 64
NEG_INF = -1e30
FOURIER_GROUPS = 8
FOURIER_GROUP_DIM = D_MODEL // FOURIER_GROUPS
N_EXPERT_GROUPS = 8
EXPERTS_PER_GROUP = 8
N_EXPERTS = N_EXPERT_GROUPS * EXPERTS_PER_GROUP
TOP_K_INNER = 2
D_EXPERT = D_MODEL // 4
MOE_BLOCK = 128
EPS = 1e-6
N_ATTN_LAYERS = (DEPTH + 1) // 2
N_FOURIER_LAYERS = DEPTH // 2

kernel_name = "hybrid_dilated_fourier_hmoe_encoder"


def rmsnorm(x, g):
    xf = x.astype(jnp.float32)
    y = xf * lax.rsqrt(jnp.mean(xf * xf, axis=-1, keepdims=True) + EPS)
    return (y * g.astype(jnp.float32)).astype(x.dtype)


def partial_rope(t, seq_len):
    half = ROT_DIM // 2
    inv_freq = ROPE_THETA ** (-jnp.arange(0, ROT_DIM, 2, dtype=jnp.float32) / ROT_DIM)
    ang = jnp.arange(seq_len, dtype=jnp.float32)[:, None] * inv_freq[None, :]
    cos = jnp.cos(ang)[None, :, None, :]
    sin = jnp.sin(ang)[None, :, None, :]
    tf = t.astype(jnp.float32)
    t1, t2, rest = tf[..., :half], tf[..., half:ROT_DIM], tf[..., ROT_DIM:]
    out = jnp.concatenate([t1 * cos - t2 * sin, t2 * cos + t1 * sin, rest], axis=-1)
    return out.astype(t.dtype)


def dilated_window_attention(q, k, v, window, dilation):
    b_, s_, h_, dh = q.shape
    radius = (window // 2) // dilation
    length = s_ // dilation
    nb = -(-length // ATTN_BLOCK)
    lp = nb * ATTN_BLOCK
    kspan = ATTN_BLOCK + 2 * radius

    def phase(t):
        return t.reshape(b_, length, dilation, h_, dh).transpose(0, 2, 3, 1, 4).astype(jnp.float32)

    qs, ks, vs = phase(q), phase(k), phase(v)
    qb = jnp.pad(qs, ((0, 0), (0, 0), (0, 0), (0, lp - length), (0, 0)))
    qb = qb.reshape(b_, dilation, h_, nb, ATTN_BLOCK, dh)
    pad_kv = ((0, 0), (0, 0), (0, 0), (radius, lp - length + radius), (0, 0))
    idx = np.arange(nb)[:, None] * ATTN_BLOCK + np.arange(kspan)[None, :]
    kb = jnp.take(jnp.pad(ks, pad_kv), idx, axis=3)
    vb = jnp.take(jnp.pad(vs, pad_kv), idx, axis=3)

    jq = np.arange(nb)[:, None] * ATTN_BLOCK + np.arange(ATTN_BLOCK)[None, :]
    jk = np.arange(nb)[:, None] * ATTN_BLOCK + np.arange(kspan)[None, :] - radius
    valid = (np.abs(jk[:, None, :] - jq[:, :, None]) <= radius) & (jk[:, None, :] >= 0) & (jk[:, None, :] < length)

    s = jnp.einsum('bdhnqc,bdhnkc->bdhnqk', qb, kb) * (dh ** -0.5)
    s = jnp.where(jnp.asarray(valid), s, NEG_INF)
    m = jnp.max(s, axis=-1, keepdims=True)
    p = jnp.exp(s - m)
    l = jnp.sum(p, axis=-1, keepdims=True)
    o = jnp.einsum('bdhnqk,bdhnkc->bdhnqc', p, vb) / l
    lse = (m + jnp.log(l))[..., 0]
    o = o.reshape(b_, dilation, h_, lp, dh)[:, :, :, :length]
    o = o.transpose(0, 3, 1, 2, 4).reshape(b_, s_, h_, dh)
    lse = lse.reshape(b_, dilation, h_, lp)[..., :length].transpose(0, 3, 1, 2).reshape(b_, s_, h_)
    return o, lse


def dilated_attention_mixer(x, norm_g, w_qkv, q_g, k_g, w_out):
    b_, s_, _ = x.shape
    h = rmsnorm(x, norm_g)
    qkv = jnp.einsum('bsd,de->bse', h, w_qkv)
    q, k, v = jnp.split(qkv, 3, axis=-1)
    q = q.reshape(b_, s_, N_HEADS, HEAD_DIM)
    k = k.reshape(b_, s_, N_HEADS, HEAD_DIM)
    v = v.reshape(b_, s_, N_HEADS, HEAD_DIM)
    q = partial_rope(rmsnorm(q, q_g), s_)
    k = partial_rope(rmsnorm(k, k_g), s_)
    outs, lses = [], []
    for gi, (window, dilation) in enumerate(DILATED_PATTERNS):
        sl = slice(gi * HEADS_PER_GROUP, (gi + 1) * HEADS_PER_GROUP)
        o, lse = dilated_window_attention(q[:, :, sl], k[:, :, sl], v[:, :, sl], window, dilation)
        outs.append(o)
        lses.append(lse)
    outs = jnp.stack(outs, axis=2)
    alpha = jax.nn.softmax(jnp.stack(lses, axis=2), axis=2)
    mixed = (outs * alpha[..., None]).reshape(b_, s_, ATTN_WIDTH).astype(x.dtype)
    return jnp.einsum('bse,ed->bsd', mixed, w_out)


def fourier_mixer(x, norm_g, w_in, w_out):
    b_, s_, _ = x.shape
    h = rmsnorm(x, norm_g)
    u = jnp.einsum('bsd,de->bse', h, w_in).astype(jnp.float32)
    u = u.reshape(b_, s_, FOURIER_GROUPS, FOURIER_GROUP_DIM)
    f = jnp.fft.fft2(u, axes=(1, 3), norm='ortho').real
    f = f.reshape(b_, s_, D_MODEL).astype(x.dtype)
    return jnp.einsum('bse,ed->bsd', f, w_out)


def hierarchical_moe(x, norm_g, w_rg, b_rg, w_re, b_re, w_gate, w_up, w_down):
    b_, s_, d_ = x.shape
    n = b_ * s_
    h = rmsnorm(x, norm_g).reshape(n, d_)
    hf = h.astype(jnp.float32)
    coarse = hf @ w_rg.astype(jnp.float32) + b_rg.astype(jnp.float32)
    g_sel = jnp.argmax(coarse, axis=-1)
    g_gate = jnp.take_along_axis(jax.nn.softmax(coarse, axis=-1), g_sel[:, None], axis=-1)
    fine = (hf @ w_re.astype(jnp.float32) + b_re.astype(jnp.float32)).reshape(n, N_EXPERT_GROUPS, EXPERTS_PER_GROUP)
    fine = jnp.take_along_axis(fine, g_sel[:, None, None], axis=1)[:, 0]
    top_v, top_i = lax.top_k(fine, TOP_K_INNER)
    weights = g_gate * jax.nn.softmax(top_v, axis=-1)
    experts = g_sel[:, None] * EXPERTS_PER_GROUP + top_i

    n_assign = n * TOP_K_INNER
    e_flat = experts.reshape(-1).astype(jnp.int32)
    w_flat = weights.reshape(-1)
    tok_flat = jnp.repeat(jnp.arange(n, dtype=jnp.int32), TOP_K_INNER)
    counts = jnp.zeros((N_EXPERTS,), jnp.int32).at[e_flat].add(1)
    padded = ((counts + MOE_BLOCK - 1) // MOE_BLOCK) * MOE_BLOCK
    pend = jnp.cumsum(padded)
    pstart = pend - padded
    start = jnp.cumsum(counts) - counts
    order = jnp.argsort(e_flat, stable=True)
    se = e_flat[order]
    dest = pstart[se] + (jnp.arange(n_assign, dtype=jnp.int32) - start[se])
    n_blocks = -(-n_assign // MOE_BLOCK) + N_EXPERTS
    cap = n_blocks * MOE_BLOCK
    slot_tok = jnp.full((cap,), n, jnp.int32).at[dest].set(tok_flat[order])
    slot_w = jnp.zeros((cap,), jnp.float32).at[dest].set(w_flat[order])
    block_start = jnp.arange(n_blocks, dtype=jnp.int32) * MOE_BLOCK
    block_e = jnp.minimum(jnp.searchsorted(pend, block_start, side='right'), N_EXPERTS - 1)

    h_pad = jnp.concatenate([h, jnp.zeros((1, d_), h.dtype)], axis=0)
    xs = h_pad[slot_tok].reshape(n_blocks, MOE_BLOCK, d_)

    def expert_block(args):
        xb, e = args
        a = jax.nn.silu(xb @ w_gate[e]) * (xb @ w_up[e])
        return a @ w_down[e]

    ys = lax.map(expert_block, (xs, block_e)).reshape(cap, d_)
    ys = ys * slot_w[:, None].astype(ys.dtype)
    out = jnp.zeros((n + 1, d_), ys.dtype).at[slot_tok].add(ys)[:n]
    return out.reshape(b_, s_, d_).astype(x.dtype)


def setup_inputs(seed: int = 0) -> dict:
    key = jax.random.key(seed)
    ks = jax.random.split(key, 20)
    f32 = jnp.float32
    nrm = lambda k, shape, scale: jax.random.normal(k, shape, f32) * scale
    gain = lambda k, shape: 1.0 + 0.02 * jax.random.normal(k, shape, f32)
    return {
        "x": jax.random.normal(ks[0], (BATCH, SEQ, D_MODEL), f32),
        "attn_norm_g": gain(ks[1], (N_ATTN_LAYERS, D_MODEL)),
        "w_qkv": nrm(ks[2], (N_ATTN_LAYERS, D_MODEL, 3 * ATTN_WIDTH), D_MODEL ** -0.5),
        "q_norm_g": gain(ks[3], (N_ATTN_LAYERS, HEAD_DIM)),
        "k_norm_g": gain(ks[4], (N_ATTN_LAYERS, HEAD_DIM)),
        "w_attn_out": nrm(ks[5], (N_ATTN_LAYERS, ATTN_WIDTH, D_MODEL), ATTN_WIDTH ** -0.5),
        "fourier_norm_g": gain(ks[6], (N_FOURIER_LAYERS, D_MODEL)),
        "w_fourier_in": nrm(ks[7], (N_FOURIER_LAYERS, D_MODEL, D_MODEL), D_MODEL ** -0.5),
        "w_fourier_out": nrm(ks[8], (N_FOURIER_LAYERS, D_MODEL, D_MODEL), D_MODEL ** -0.5),
        "moe_norm_g": gain(ks[9], (DEPTH, D_MODEL)),
        "w_router_group": nrm(ks[10], (DEPTH, D_MODEL, N_EXPERT_GROUPS), D_MODEL ** -0.5),
        "b_router_group": nrm(ks[11], (DEPTH, N_EXPERT_GROUPS), 0.01),
        "w_router_expert": nrm(ks[12], (DEPTH, D_MODEL, N_EXPERTS), D_MODEL ** -0.5),
        "b_router_expert": nrm(ks[13], (DEPTH, N_EXPERTS), 0.01),
        "w_expert_gate": nrm(ks[14], (DEPTH, N_EXPERTS, D_MODEL, D_EXPERT), D_MODEL ** -0.5),
        "w_expert_up": nrm(ks[15], (DEPTH, N_EXPERTS, D_MODEL, D_EXPERT), D_MODEL ** -0.5),
        "w_expert_down": nrm(ks[16], (DEPTH, N_EXPERTS, D_EXPERT, D_MODEL), D_EXPERT ** -0.5),
    }


def reference(x, attn_norm_g, w_qkv, q_norm_g, k_norm_g, w_attn_out,
              fourier_norm_g, w_fourier_in, w_fourier_out,
              moe_norm_g, w_router_group, b_router_group, w_router_expert, b_router_expert,
              w_expert_gate, w_expert_up, w_expert_down):
    for i in range(DEPTH):
        j = i // N_MIXERS
        if i % N_MIXERS == 0:
            x = x + dilated_attention_mixer(x, attn_norm_g[j], w_qkv[j], q_norm_g[j], k_norm_g[j], w_attn_out[j])
        else:
            x = x + fourier_mixer(x, fourier_norm_g[j], w_fourier_in[j], w_fourier_out[j])
        x = x + hierarchical_moe(x, moe_norm_g[i], w_router_group[i], b_router_group[i],
                                 w_router_expert[i], b_router_expert[i],
                                 w_expert_gate[i], w_expert_up[i], w_expert_down[i])
    return x
```

```python
import functools

import numpy as np
import jax
import jax.numpy as jnp
from jax import lax
from jax.experimental import pallas as pl
from jax.experimental.pallas import tpu as pltpu

F32 = jnp.float32
BF16 = jnp.bfloat16

HEAD_DIM = 128
HEADS_PER_GROUP = 4
DILATED_PATTERNS = ((128, 1), (512, 4), (2048, 16))
N_ATTN_GROUPS = len(DILATED_PATTERNS)
GROUP_WIDTH = HEADS_PER_GROUP * HEAD_DIM
ROT_DIM = HEAD_DIM // 4
ROPE_THETA = 500000.0
NEG_INF = -1e30
FOURIER_GROUPS = 8
N_EXPERT_GROUPS = 8
EXPERTS_PER_GROUP = 8
N_EXPERTS = N_EXPERT_GROUPS * EXPERTS_PER_GROUP
TOP_K = 2
MOE_BLOCK = 128
EPS = 1e-6

LANES = 128
ATTN_Q_BLOCK = 128
ROW_TILE = 512
VMEM_LIMIT = 48 * 1024 * 1024


def _params(*sem):
    return pltpu.CompilerParams(dimension_semantics=sem, vmem_limit_bytes=VMEM_LIMIT)


def _rms(x, g):
    ms = jnp.mean(x * x, axis=-1, keepdims=True)
    return x * lax.rsqrt(ms + EPS) * g


def _qkv_kernel(x_ref, g_ref, w_ref, qg_ref, kg_ref, cos_ref, sa_ref, sb_ref, o_ref, h_ref):
    @pl.when(pl.program_id(1) == 0)
    def _():
        h_ref[...] = _rms(x_ref[...], g_ref[...]).astype(BF16)

    r = jnp.dot(h_ref[...], w_ref[...], preferred_element_type=F32)
    cos, sa, sb = cos_ref[...], sa_ref[...], sb_ref[...]
    for j in range(2 * HEADS_PER_GROUP):
        t = r[:, j * HEAD_DIM:(j + 1) * HEAD_DIM]
        t = _rms(t, qg_ref[...] if j < HEADS_PER_GROUP else kg_ref[...])
        t = t * cos + pltpu.roll(t, ROT_DIM // 2, 1) * sa + pltpu.roll(t, HEAD_DIM - ROT_DIM // 2, 1) * sb
        o_ref[:, j * HEAD_DIM:(j + 1) * HEAD_DIM] = t.astype(BF16)
    o_ref[:, 2 * GROUP_WIDTH:] = r[:, 2 * GROUP_WIDTH:].astype(BF16)


def _qkv_project(x2d, g, w3, qg, kg, cos_t, sa_t, sb_t, seq):
    n, d = x2d.shape
    tm = ROW_TILE
    seq_tiles = seq // tm
    gw3 = 3 * GROUP_WIDTH
    tab = pl.BlockSpec((tm, LANES), lambda i, j: (i % seq_tiles, 0))
    return pl.pallas_call(
        _qkv_kernel,
        out_shape=jax.ShapeDtypeStruct((N_ATTN_GROUPS, n, gw3), BF16),
        grid=(n // tm, N_ATTN_GROUPS),
        in_specs=[
            pl.BlockSpec((tm, d), lambda i, j: (i, 0)),
            pl.BlockSpec((1, d), lambda i, j: (0, 0)),
            pl.BlockSpec((None, d, gw3), lambda i, j: (j, 0, 0)),
            pl.BlockSpec((1, LANES), lambda i, j: (0, 0)),
            pl.BlockSpec((1, LANES), lambda i, j: (0, 0)),
            tab, tab, tab,
        ],
        out_specs=pl.BlockSpec((None, tm, gw3), lambda i, j: (j, i, 0)),
        scratch_shapes=[pltpu.VMEM((tm, d), BF16)],
        compiler_params=_params("parallel", "arbitrary"),
        name="qkv_project",
    )(x2d, g, w3, qg, kg, cos_t, sa_t, sb_t)


def _attn_kernel(qkv_ref, o_ref, lse_ref, *, dilation, length, radius):
    gw = GROUP_WIDTH
    bq = ATTN_Q_BLOCK
    win = min(length, bq + 2 * radius)
    lane = lax.broadcasted_iota(jnp.int32, (bq, LANES), 1)
    for r in range(dilation):
        base = r * 3 * gw
        for qb in range(length // bq):
            q0 = qb * bq
            k0 = min(max(q0 - radius, 0), length - win)
            jq = q0 + lax.broadcasted_iota(jnp.int32, (bq, win), 0)
            jk = k0 + lax.broadcasted_iota(jnp.int32, (bq, win), 1)
            valid = jnp.abs(jk - jq) <= radius
            lse_tile = jnp.zeros((bq, LANES), F32)
            for hh in range(HEADS_PER_GROUP):
                c = base + hh * HEAD_DIM
                q = qkv_ref[q0:q0 + bq, c:c + HEAD_DIM]
                k = qkv_ref[k0:k0 + win, c + gw:c + gw + HEAD_DIM]
                v = qkv_ref[k0:k0 + win, c + 2 * gw:c + 2 * gw + HEAD_DIM]
                s = lax.dot_general(q, k, (((1,), (1,)), ((), ())), preferred_element_type=F32)
                s = jnp.where(valid, s, NEG_INF)
                m = jnp.max(s, axis=-1, keepdims=True)
                p = jnp.exp(s - m)
                l = jnp.sum(p, axis=-1, keepdims=True)
                o = jnp.dot(p.astype(BF16), v, preferred_element_type=F32) / l
                oc = r * gw + hh * HEAD_DIM
                o_ref[q0:q0 + bq, oc:oc + HEAD_DIM] = o.astype(BF16)
                lse_tile = jnp.where(lane == hh, m + jnp.log(l), lse_tile)
            lse_ref[q0:q0 + bq, r * LANES:(r + 1) * LANES] = lse_tile


def _attention_group(qkv, gi, batch, seq):
    window, dilation = DILATED_PATTERNS[gi]
    radius = (window // 2) // dilation
    length = seq // dilation
    gw3 = 3 * GROUP_WIDTH
    view = qkv.reshape(N_ATTN_GROUPS, batch, length, dilation * gw3)
    kern = functools.partial(_attn_kernel, dilation=dilation, length=length, radius=radius)
    o, lse = pl.pallas_call(
        kern,
        out_shape=(jax.ShapeDtypeStruct((batch, length, dilation * GROUP_WIDTH), BF16),
                   jax.ShapeDtypeStruct((batch, length, dilation * LANES), F32)),
        grid=(batch,),
        in_specs=[pl.BlockSpec((None, None, length, dilation * gw3), lambda b: (gi, b, 0, 0))],
        out_specs=(pl.BlockSpec((None, length, dilation * GROUP_WIDTH), lambda b: (b, 0, 0)),
                   pl.BlockSpec((None, length, dilation * LANES), lambda b: (b, 0, 0))),
        compiler_params=_params("parallel"),
        name=f"band_attention_d{dilation}",
    )(view)
    n = batch * seq
    return o.reshape(n, GROUP_WIDTH), lse.reshape(n, LANES)


def _attn_out_kernel(o0, o1, o2, l0, l1, l2, x_ref, w_ref, out_ref, mix_ref):
    ls = [l0[...], l1[...], l2[...]]
    m = jnp.maximum(jnp.maximum(ls[0], ls[1]), ls[2])
    es = [jnp.exp(l - m) for l in ls]
    den = es[0] + es[1] + es[2]
    for gi, o_ref in enumerate((o0, o1, o2)):
        alpha = es[gi] / den
        for hh in range(HEADS_PER_GROUP):
            c = hh * HEAD_DIM
            a = alpha[:, hh:hh + 1]
            mix_ref[:, gi * GROUP_WIDTH + c:gi * GROUP_WIDTH + c + HEAD_DIM] = (
                o_ref[:, c:c + HEAD_DIM].astype(F32) * a).astype(BF16)
    out_ref[...] = x_ref[...] + jnp.dot(mix_ref[...], w_ref[...], preferred_element_type=F32)


def _attn_out_project(os_, lses, x2d, w_out):
    n, d = x2d.shape
    tm = ROW_TILE
    width = N_ATTN_GROUPS * GROUP_WIDTH
    ospec = pl.BlockSpec((tm, GROUP_WIDTH), lambda i: (i, 0))
    lspec = pl.BlockSpec((tm, LANES), lambda i: (i, 0))
    return pl.pallas_call(
        _attn_out_kernel,
        out_shape=jax.ShapeDtypeStruct((n, d), F32),
        grid=(n // tm,),
        in_specs=[ospec, ospec, ospec, lspec, lspec, lspec,
                  pl.BlockSpec((tm, d), lambda i: (i, 0)),
                  pl.BlockSpec((width, d), lambda i: (0, 0))],
        out_specs=pl.BlockSpec((tm, d), lambda i: (i, 0)),
        scratch_shapes=[pltpu.VMEM((tm, width), BF16)],
        compiler_params=_params("parallel"),
        name="attn_out_project",
    )(*os_, *lses, x2d, w_out)


def _router_kernel(x_ref, g_ref, w_ref, b_ref, sel_ref):
    h = _rms(x_ref[...], g_ref[...]).astype(BF16)
    logits = jnp.dot(h, w_ref[...], preferred_element_type=F32) + b_ref[...]
    lane = lax.broadcasted_iota(jnp.int32, logits.shape, 1)
    lanef = lane.astype(F32)
    big = float(LANES)
    is_grp = lane < N_EXPERT_GROUPS
    coarse = jnp.where(is_grp, logits, -jnp.inf)
    cmax = jnp.max(coarse, axis=-1, keepdims=True)
    g_sel = jnp.min(jnp.where(coarse == cmax, lanef, big), axis=-1, keepdims=True)
    den = jnp.sum(jnp.where(is_grp, jnp.exp(logits - cmax), 0.0), axis=-1, keepdims=True)
    g_gate = 1.0 / den
    lo = N_EXPERT_GROUPS + g_sel * EXPERTS_PER_GROUP
    in_grp = (lanef >= lo) & (lanef < lo + EXPERTS_PER_GROUP)
    fine = jnp.where(in_grp, logits, -jnp.inf)
    v1 = jnp.max(fine, axis=-1, keepdims=True)
    i1 = jnp.min(jnp.where(fine == v1, lanef, big), axis=-1, keepdims=True)
    fine2 = jnp.where(lanef == i1, -jnp.inf, fine)
    v2 = jnp.max(fine2, axis=-1, keepdims=True)
    i2 = jnp.min(jnp.where(fine2 == v2, lanef, big), axis=-1, keepdims=True)
    e2 = jnp.exp(v2 - v1)
    w1 = g_gate * (1.0 / (1.0 + e2))
    w2 = g_gate * (e2 / (1.0 + e2))
    sel = jnp.where(lane == 0, w1, 0.0)
    sel = jnp.where(lane == 1, w2, sel)
    sel = jnp.where(lane == 2, i1 - N_EXPERT_GROUPS, sel)
    sel = jnp.where(lane == 3, i2 - N_EXPERT_GROUPS, sel)
    sel_ref[...] = sel


def _route(x2d, g, w_r, b_r):
    n, d = x2d.shape
    tm = ROW_TILE
    return pl.pallas_call(
        _router_kernel,
        out_shape=jax.ShapeDtypeStruct((n, LANES), F32),
        grid=(n // tm,),
        in_specs=[pl.BlockSpec((tm, d), lambda i: (i, 0)),
                  pl.BlockSpec((1, d), lambda i: (0, 0)),
                  pl.BlockSpec((d, LANES), lambda i: (0, 0)),
                  pl.BlockSpec((1, LANES), lambda i: (0, 0))],
        out_specs=pl.BlockSpec((tm, LANES), lambda i: (i, 0)),
        compiler_params=_params("parallel"),
        name="moe_router",
    )(x2d, g, w_r, b_r)


def _expert_kernel(be_ref, nu_ref,
                   src_ref, nxt_ref, dst_ref,
                   x_hbm, g_ref, wg_ref, wu_ref, wd_ref, sw_ref,
                   buf_hbm, xbuf, ybuf, gsem, ssem):
    i = pl.program_id(0)
    n_used = nu_ref[0]
    slot = lax.rem(i, 2)
    rows = MOE_BLOCK

    def start_gather(idx_ref, s):
        for r in range(rows):
            pltpu.make_async_copy(x_hbm.at[pl.ds(idx_ref[0, 0, r], 1)], xbuf.at[s, pl.ds(r, 1)],
                                  gsem.at[s]).start()

    def wait_gather(s):
        pltpu.make_async_copy(x_hbm.at[pl.ds(0, rows)], xbuf.at[s], gsem.at[s]).wait()

    def wait_scatter():
        pltpu.make_async_copy(ybuf, buf_hbm.at[pl.ds(0, rows)], ssem.at[0]).wait()

    @pl.when(i < n_used)
    def _():
        @pl.when(i == 0)
        def _():
            start_gather(src_ref, 0)
            ybuf[...] = jnp.zeros_like(ybuf)
            tail = pltpu.make_async_copy(ybuf, buf_hbm.at[pl.ds(buf_hbm.shape[0] - rows, rows)], ssem.at[0])
            tail.start()
            tail.wait()

        wait_gather(slot)

        @pl.when(i + 1 < n_used)
        def _():
            start_gather(nxt_ref, 1 - slot)

        h = _rms(xbuf[slot], g_ref[...]).astype(BF16)
        gate = jnp.dot(h, wg_ref[...].astype(BF16), preferred_element_type=F32)
        up = jnp.dot(h, wu_ref[...].astype(BF16), preferred_element_type=F32)
        act = (gate * (1.0 / (1.0 + jnp.exp(-gate))) * up).astype(BF16)
        y = jnp.dot(act, wd_ref[...].astype(BF16), preferred_element_type=F32) * sw_ref[...]

        @pl.when(i > 0)
        def _():
            wait_scatter()

        ybuf[...] = y
        for r in range(rows):
            pltpu.make_async_copy(ybuf.at[pl.ds(r, 1)], buf_hbm.at[pl.ds(dst_ref[0, 0, r], 1)],
                                  ssem.at[0]).start()

        @pl.when(i == n_used - 1)
        def _():
            wait_scatter()


def _expert_mlp(x2d, g, w_gate, w_up, w_down, layer, block_e, n_used, slot_src, slot_dst, slot_w):
    n, d = x2d.shape
    d_exp = w_gate.shape[-1]
    n_blocks = block_e.shape[0]
    rows = MOE_BLOCK
    src3 = slot_src.reshape(n_blocks, 1, rows)
    dst3 = slot_dst.reshape(n_blocks, 1, rows)
    sw3 = slot_w.reshape(n_blocks, rows, 1)
    smem = functools.partial(pl.BlockSpec, (1, 1, rows), memory_space=pltpu.SMEM)
    grid_spec = pltpu.PrefetchScalarGridSpec(
        num_scalar_prefetch=2,
        grid=(n_blocks,),
        in_specs=[
            smem(lambda i, be, nu: (i, 0, 0)),
            smem(lambda i, be, nu: (jnp.minimum(i + 1, n_blocks - 1), 0, 0)),
            smem(lambda i, be, nu: (i, 0, 0)),
            pl.BlockSpec(memory_space=pl.ANY),
            pl.BlockSpec((1, d), lambda i, be, nu: (0, 0)),
            pl.BlockSpec((None, None, d, d_exp), lambda i, be, nu: (layer, be[i], 0, 0)),
            pl.BlockSpec((None, None, d, d_exp), lambda i, be, nu: (layer, be[i], 0, 0)),
            pl.BlockSpec((None, None, d_exp, d), lambda i, be, nu: (layer, be[i], 0, 0)),
            pl.BlockSpec((None, rows, 1), lambda i, be, nu: (i, 0, 0)),
        ],
        out_specs=pl.BlockSpec(memory_space=pl.ANY),
        scratch_shapes=[pltpu.VMEM((2, rows, d), F32), pltpu.VMEM((rows, d), F32),
                        pltpu.SemaphoreType.DMA((2,)), pltpu.SemaphoreType.DMA((1,))],
    )
    return pl.pallas_call(
        _expert_kernel,
        out_shape=jax.ShapeDtypeStruct((TOP_K * n + rows, d), F32),
        grid_spec=grid_spec,
        compiler_params=_params("arbitrary"),
        name="moe_expert_mlp",
    )(block_e, n_used, src3, src3, dst3, x2d, g, w_gate, w_up, w_down, sw3)


def _dispatch_plan(sel, n):
    n_assign = n * TOP_K
    n_blocks = -(-n_assign // MOE_BLOCK) + N_EXPERTS
    w_flat = sel[:, 0:TOP_K].reshape(-1)
    e_flat = sel[:, TOP_K:2 * TOP_K].astype(jnp.int32).reshape(-1)
    experts = jnp.arange(N_EXPERTS, dtype=jnp.int32)
    counts = jnp.sum((e_flat[:, None] == experts[None, :]).astype(jnp.int32), axis=0)
    padded = ((counts + MOE_BLOCK - 1) // MOE_BLOCK) * MOE_BLOCK
    pend = jnp.cumsum(padded)
    pstart = pend - padded
    start = jnp.cumsum(counts) - counts
    order = jnp.argsort(e_flat, stable=True).astype(jnp.int32)
    n_used = pend[-1] // MOE_BLOCK
    blk = jnp.arange(n_blocks, dtype=jnp.int32)
    blk_c = jnp.minimum(blk, n_used - 1)
    block_e = jnp.sum((pend[None, :] <= (blk_c * MOE_BLOCK)[:, None]).astype(jnp.int32), axis=1)
    block_e = jnp.minimum(block_e, N_EXPERTS - 1)
    first_row = blk * MOE_BLOCK - pstart[block_e]
    n_valid = jnp.where(blk < n_used, jnp.clip(counts[block_e] - first_row, 0, MOE_BLOCK), 0)
    in_blk = jnp.arange(MOE_BLOCK, dtype=jnp.int32)[None, :]
    row = first_row[:, None] + in_blk
    valid = in_blk < n_valid[:, None]
    a = order[jnp.clip(start[block_e][:, None] + row, 0, n_assign - 1)]
    tok = a // TOP_K
    slot_src = jnp.where(valid, tok, 0)
    slot_dst = jnp.where(valid, (a % TOP_K) * n + tok, TOP_K * n + in_blk)
    slot_w = jnp.where(valid, w_flat[a], 0.0)
    return (block_e.astype(jnp.int32), n_used.reshape(1).astype(jnp.int32),
            slot_src.astype(jnp.int32), slot_dst.astype(jnp.int32), slot_w.astype(F32))


def _combine_kernel(x_ref, y0_ref, y1_ref, o_ref):
    o_ref[...] = x_ref[...] + (y0_ref[...] + y1_ref[...])


def _combine(x2d, buf):
    n, d = x2d.shape
    tm = ROW_TILE
    tiles = n // tm
    return pl.pallas_call(
        _combine_kernel,
        out_shape=jax.ShapeDtypeStruct((n, d), F32),
        grid=(tiles,),
        in_specs=[pl.BlockSpec((tm, d), lambda i: (i, 0)),
                  pl.BlockSpec((tm, d), lambda i: (i, 0)),
                  pl.BlockSpec((tm, d), lambda i: (i + tiles, 0))],
        out_specs=pl.BlockSpec((tm, d), lambda i: (i, 0)),
        compiler_params=_params("parallel"),
        name="moe_combine",
    )(x2d, buf, buf)


def _moe(x2d, g, w_rg, b_rg, w_re, b_re, w_gate, w_up, w_down, layer):
    n, d = x2d.shape
    pad = LANES - N_EXPERT_GROUPS - N_EXPERTS
    w_r = jnp.concatenate([w_rg, w_re, jnp.zeros((d, pad), F32)], axis=1).astype(BF16)
    b_r = jnp.concatenate([b_rg, b_re, jnp.zeros((pad,), F32)]).reshape(1, LANES)
    g2 = g.reshape(1, d)
    sel = _route(x2d, g2, w_r, b_r)
    plan = _dispatch_plan(sel, n)
    buf = _expert_mlp(x2d, g2, w_gate, w_up, w_down, layer, *plan)
    return _combine(x2d, buf)


def _fourier_in_kernel(x_ref, g_ref, w_ref, cc_ref, sc_ref, a_ref, b_ref):
    h = _rms(x_ref[...], g_ref[...]).astype(BF16)
    u = jnp.dot(h, w_ref[...], preferred_element_type=F32).astype(BF16)
    gd = cc_ref.shape[0]
    for gi in range(FOURIER_GROUPS):
        ug = u[:, gi * gd:(gi + 1) * gd]
        a_ref[:, gi * gd:(gi + 1) * gd] = jnp.dot(ug, cc_ref[...], preferred_element_type=F32).astype(BF16)
        b_ref[:, gi * gd:(gi + 1) * gd] = jnp.dot(ug, sc_ref[...], preferred_element_type=F32).astype(BF16)


def _fourier_in(x2d, g, w_in, cc, sc):
    n, d = x2d.shape
    tm = ROW_TILE
    gd = d // FOURIER_GROUPS
    out = jax.ShapeDtypeStruct((n, d), BF16)
    return pl.pallas_call(
        _fourier_in_kernel,
        out_shape=(out, out),
        grid=(n // tm,),
        in_specs=[pl.BlockSpec((tm, d), lambda i: (i, 0)),
                  pl.BlockSpec((1, d), lambda i: (0, 0)),
                  pl.BlockSpec((d, d), lambda i: (0, 0)),
                  pl.BlockSpec((gd, gd), lambda i: (0, 0)),
                  pl.BlockSpec((gd, gd), lambda i: (0, 0))],
        out_specs=(pl.BlockSpec((tm, d), lambda i: (i, 0)), pl.BlockSpec((tm, d), lambda i: (i, 0))),
        compiler_params=_params("parallel"),
        name="fourier_in",
    )(x2d, g, w_in, cc, sc)


def _seq_dft_kernel(cs_ref, ss_ref, a_ref, b_ref, f_ref):
    f = jnp.dot(cs_ref[...], a_ref[...], preferred_element_type=F32)
    f = f + jnp.dot(ss_ref[...], b_ref[...], preferred_element_type=F32)
    f_ref[...] = f.astype(BF16)


def _seq_dft(cs, ss, a3, b3):
    batch, seq, d = a3.shape
    tm = min(seq, 1024)
    tn = min(d, 1024)
    return pl.pallas_call(
        _seq_dft_kernel,
        out_shape=jax.ShapeDtypeStruct((batch, seq, d), BF16),
        grid=(batch, d // tn, seq // tm),
        in_specs=[pl.BlockSpec((tm, seq), lambda b, j, i: (i, 0)),
                  pl.BlockSpec((tm, seq), lambda b, j, i: (i, 0)),
                  pl.BlockSpec((None, seq, tn), lambda b, j, i: (b, 0, j)),
                  pl.BlockSpec((None, seq, tn), lambda b, j, i: (b, 0, j))],
        out_specs=pl.BlockSpec((None, tm, tn), lambda b, j, i: (b, i, j)),
        compiler_params=_params("parallel", "parallel", "arbitrary"),
        name="seq_dft",
    )(cs, ss, a3, b3)


def _proj_residual_kernel(f_ref, w_ref, x_ref, o_ref):
    o_ref[...] = x_ref[...] + jnp.dot(f_ref[...], w_ref[...], preferred_element_type=F32)


def _proj_residual(f2d, w, x2d):
    n, d = x2d.shape
    k = f2d.shape[1]
    tm = ROW_TILE
    return pl.pallas_call(
        _proj_residual_kernel,
        out_shape=jax.ShapeDtypeStruct((n, d), F32),
        grid=(n // tm,),
        in_specs=[pl.BlockSpec((tm, k), lambda i: (i, 0)),
                  pl.BlockSpec((k, d), lambda i: (0, 0)),
                  pl.BlockSpec((tm, d), lambda i: (i, 0))],
        out_specs=pl.BlockSpec((tm, d), lambda i: (i, 0)),
        compiler_params=_params("parallel"),
        name="proj_residual",
    )(f2d, w, x2d)


def _rope_tables(seq):
    half = ROT_DIM // 2
    inv_freq = ROPE_THETA ** (-np.arange(0, ROT_DIM, 2, dtype=np.float64) / ROT_DIM)
    ang = np.arange(seq, dtype=np.float64)[:, None] * inv_freq[None, :]
    cos_t = np.ones((seq, LANES))
    sa = np.zeros((seq, LANES))
    sb = np.zeros((seq, LANES))
    cos_t[:, :half] = np.cos(ang)
    cos_t[:, half:ROT_DIM] = np.cos(ang)
    sa[:, half:ROT_DIM] = np.sin(ang)
    sb[:, :half] = -np.sin(ang)
    return [jnp.asarray(t, F32) for t in (cos_t, sa, sb)]


def _dft_tables(n):
    jk = (np.arange(n, dtype=np.int64)[:, None] * np.arange(n, dtype=np.int64)[None, :]) % n
    ang = 2.0 * np.pi * jk.astype(np.float64) / n
    scale = n ** -0.5
    return np.cos(ang) * scale, np.sin(ang) * scale


def _attention_mixer(x2d, batch, seq, norm_g, w_qkv, q_g, k_g, w_out):
    n, d = x2d.shape
    width = N_ATTN_GROUPS * GROUP_WIDTH
    gw = GROUP_WIDTH
    w3 = jnp.stack([jnp.concatenate([w_qkv[:, t * width + gi * gw:t * width + (gi + 1) * gw]
                                     for t in range(3)], axis=1)
                    for gi in range(N_ATTN_GROUPS)]).astype(BF16)
    qg = (q_g * HEAD_DIM ** -0.5).reshape(1, HEAD_DIM)
    kg = k_g.reshape(1, HEAD_DIM)
    cos_t, sa_t, sb_t = _rope_tables(seq)
    qkv = _qkv_project(x2d, norm_g.reshape(1, d), w3, qg, kg, cos_t, sa_t, sb_t, seq)
    outs = [_attention_group(qkv, gi, batch, seq) for gi in range(N_ATTN_GROUPS)]
    return _attn_out_project([o for o, _ in outs], [l for _, l in outs], x2d, w_out.astype(BF16))


def _fourier_mixer(x2d, batch, seq, norm_g, w_in, w_out):
    n, d = x2d.shape
    gd = d // FOURIER_GROUPS
    cc, sc = _dft_tables(gd)
    cs, ss = _dft_tables(seq)
    cc, sc = jnp.asarray(cc, F32).astype(BF16), jnp.asarray(sc, F32).astype(BF16)
    cs, nss = jnp.asarray(cs, F32).astype(BF16), jnp.asarray(-ss, F32).astype(BF16)
    a, b = _fourier_in(x2d, norm_g.reshape(1, d), w_in.astype(BF16), cc, sc)
    f = _seq_dft(cs, nss, a.reshape(batch, seq, d), b.reshape(batch, seq, d))
    return _proj_residual(f.reshape(n, d), w_out.astype(BF16), x2d)


def kernel(x, attn_norm_g, w_qkv, q_norm_g, k_norm_g, w_attn_out, fourier_norm_g, w_fourier_in, w_fourier_out, moe_norm_g, w_router_group, b_router_group, w_router_expert, b_router_expert, w_expert_gate, w_expert_up, w_expert_down):
    batch, seq, d = x.shape
    depth = moe_norm_g.shape[0]
    h = x.reshape(batch * seq, d)
    for i in range(depth):
        j = i // 2
        if i % 2 == 0:
            h = _attention_mixer(h, batch, seq, attn_norm_g[j], w_qkv[j], q_norm_g[j], k_norm_g[j], w_attn_out[j])
        else:
            h = _fourier_mixer(h, batch, seq, fourier_norm_g[j], w_fourier_in[j], w_fourier_out[j])
        h = _moe(h, moe_norm_g[i], w_router_group[i], b_router_group[i], w_router_expert[i],
                 b_router_expert[i], w_expert_gate, w_expert_up, w_expert_down, i)
    return h.reshape(batch, seq, d)
```

```python
import functools

import numpy as np
import jax
import jax.numpy as jnp
from jax import lax
from jax.experimental import pallas as pl
from jax.experimental.pallas import tpu as pltpu

F32 = jnp.float32
BF16 = jnp.bfloat16

HEAD_DIM = 128
HEADS_PER_GROUP = 4
DILATED_PATTERNS = ((128, 1), (512, 4), (2048, 16))
N_ATTN_GROUPS = len(DILATED_PATTERNS)
GROUP_WIDTH = HEADS_PER_GROUP * HEAD_DIM
ROT_DIM = HEAD_DIM // 4
ROPE_THETA = 500000.0
NEG_INF = -1e30
FOURIER_GROUPS = 8
N_EXPERT_GROUPS = 8
EXPERTS_PER_GROUP = 8
N_EXPERTS = N_EXPERT_GROUPS * EXPERTS_PER_GROUP
TOP_K = 2
MOE_BLOCK = 128
EPS = 1e-6

LANES = 128
ATTN_Q_BLOCK = 128
ROW_TILE = 512
VMEM_LIMIT = 48 * 1024 * 1024


def _params(*sem):
    return pltpu.CompilerParams(dimension_semantics=sem, vmem_limit_bytes=VMEM_LIMIT)


def _rms(x, g):
    ms = jnp.mean(x * x, axis=-1, keepdims=True)
    return x * lax.rsqrt(ms + EPS) * g


def _qkv_kernel(x_ref, g_ref, w_ref, qg_ref, kg_ref, cos_ref, sa_ref, sb_ref, *rest):
    o_refs, (h_ref, r_ref) = rest[:N_ATTN_GROUPS], rest[N_ATTN_GROUPS:]
    gi_now = pl.program_id(1)

    @pl.when(gi_now == 0)
    def _():
        h_ref[...] = _rms(x_ref[...], g_ref[...]).astype(BF16)

    r = jnp.dot(h_ref[...], w_ref[...], preferred_element_type=F32)
    cos, sa, sb = cos_ref[...], sa_ref[...], sb_ref[...]
    for j in range(2 * HEADS_PER_GROUP):
        t = r[:, j * HEAD_DIM:(j + 1) * HEAD_DIM]
        t = _rms(t, qg_ref[...] if j < HEADS_PER_GROUP else kg_ref[...])
        t = t * cos + pltpu.roll(t, ROT_DIM // 2, 1) * sa + pltpu.roll(t, HEAD_DIM - ROT_DIM // 2, 1) * sb
        r_ref[j] = t
    for j in range(2 * HEADS_PER_GROUP, 3 * HEADS_PER_GROUP):
        r_ref[j] = r[:, j * HEAD_DIM:(j + 1) * HEAD_DIM]

    chunks, tm, _ = r_ref.shape
    for gi, o_ref in enumerate(o_refs):
        dilation = DILATED_PATTERNS[gi][1]

        @pl.when(gi_now == gi)
        def _(o_ref=o_ref, dilation=dilation):
            for phase in range(dilation):
                for c in range(chunks):
                    rows = r_ref[c, pl.ds(phase, tm // dilation, stride=dilation), :]
                    col = (phase * chunks + c) * LANES
                    o_ref[:, col:col + LANES] = rows.astype(BF16)


def _qkv_project(x2d, g, w3, qg, kg, cos_t, sa_t, sb_t, batch, seq):
    n, d = x2d.shape
    tm = ROW_TILE
    seq_tiles = seq // tm
    gw3 = 3 * GROUP_WIDTH
    tab = pl.BlockSpec((tm, LANES), lambda i, j: (i % seq_tiles, 0))
    dils = [dil for _, dil in DILATED_PATTERNS]
    return pl.pallas_call(
        _qkv_kernel,
        out_shape=[jax.ShapeDtypeStruct((batch, seq // dil, dil * gw3), BF16) for dil in dils],
        grid=(n // tm, N_ATTN_GROUPS),
        in_specs=[
            pl.BlockSpec((tm, d), lambda i, j: (i, 0)),
            pl.BlockSpec((1, d), lambda i, j: (0, 0)),
            pl.BlockSpec((None, d, gw3), lambda i, j: (j, 0, 0)),
            pl.BlockSpec((1, LANES), lambda i, j: (0, 0)),
            pl.BlockSpec((1, LANES), lambda i, j: (0, 0)),
            tab, tab, tab,
        ],
        out_specs=[pl.BlockSpec((None, tm // dil, dil * gw3), lambda i, j: (i // seq_tiles, i % seq_tiles, 0))
                   for dil in dils],
        scratch_shapes=[pltpu.VMEM((tm, d), BF16), pltpu.VMEM((gw3 // LANES, tm, LANES), F32)],
        compiler_params=_params("parallel", "arbitrary"),
        name="qkv_project",
    )(x2d, g, w3, qg, kg, cos_t, sa_t, sb_t)


def _attn_kernel(qkv_ref, o_ref, lse_ref, *, dilation, length, radius):
    gw = GROUP_WIDTH
    bq = ATTN_Q_BLOCK
    win = min(length, bq + 2 * radius)
    lane = lax.broadcasted_iota(jnp.int32, (bq, LANES), 1)
    for r in range(dilation):
        base = r * 3 * gw
        for qb in range(length // bq):
            q0 = qb * bq
            k0 = min(max(q0 - radius, 0), length - win)
            jq = q0 + lax.broadcasted_iota(jnp.int32, (bq, win), 0)
            jk = k0 + lax.broadcasted_iota(jnp.int32, (bq, win), 1)
            valid = jnp.abs(jk - jq) <= radius
            lse_tile = jnp.zeros((bq, LANES), F32)
            for hh in range(HEADS_PER_GROUP):
                c = base + hh * HEAD_DIM
                q = qkv_ref[q0:q0 + bq, c:c + HEAD_DIM]
                k = qkv_ref[k0:k0 + win, c + gw:c + gw + HEAD_DIM]
                v = qkv_ref[k0:k0 + win, c + 2 * gw:c + 2 * gw + HEAD_DIM]
                s = lax.dot_general(q, k, (((1,), (1,)), ((), ())), preferred_element_type=F32)
                s = jnp.where(valid, s, NEG_INF)
                m = jnp.max(s, axis=-1, keepdims=True)
                p = jnp.exp(s - m)
                l = jnp.sum(p, axis=-1, keepdims=True)
                o = jnp.dot(p.astype(BF16), v, preferred_element_type=F32) / l
                oc = r * gw + hh * HEAD_DIM
                o_ref[q0:q0 + bq, oc:oc + HEAD_DIM] = o.astype(BF16)
                lse_tile = jnp.where(lane == hh, m + jnp.log(l), lse_tile)
            lse_ref[q0:q0 + bq, r * LANES:(r + 1) * LANES] = lse_tile


def _attention_group(qkv_g, gi, batch, seq):
    window, dilation = DILATED_PATTERNS[gi]
    radius = (window // 2) // dilation
    length = seq // dilation
    gw3 = 3 * GROUP_WIDTH
    kern = functools.partial(_attn_kernel, dilation=dilation, length=length, radius=radius)
    return pl.pallas_call(
        kern,
        out_shape=(jax.ShapeDtypeStruct((batch, length, dilation * GROUP_WIDTH), BF16),
                   jax.ShapeDtypeStruct((batch, length, dilation * LANES), F32)),
        grid=(batch,),
        in_specs=[pl.BlockSpec((None, length, dilation * gw3), lambda b: (b, 0, 0))],
        out_specs=(pl.BlockSpec((None, length, dilation * GROUP_WIDTH), lambda b: (b, 0, 0)),
                   pl.BlockSpec((None, length, dilation * LANES), lambda b: (b, 0, 0))),
        compiler_params=_params("parallel"),
        name=f"band_attention_d{dilation}",
    )(qkv_g)


def _attn_out_kernel(o0, o1, o2, l0, l1, l2, x_ref, w_ref, out_ref, mix_ref, o_rows, l_rows):
    tm = out_ref.shape[0]
    for gi, (o_ref, l_ref) in enumerate(((o0, l0), (o1, l1), (o2, l2))):
        dilation = DILATED_PATTERNS[gi][1]
        for phase in range(dilation):
            dst = pl.ds(phase, tm // dilation, stride=dilation)
            for hh in range(HEADS_PER_GROUP):
                col = phase * GROUP_WIDTH + hh * HEAD_DIM
                o_rows[gi * HEADS_PER_GROUP + hh, dst, :] = o_ref[:, col:col + HEAD_DIM].astype(F32)
            l_rows[gi, dst, :] = l_ref[:, phase * LANES:(phase + 1) * LANES]
    ls = [l_rows[gi] for gi in range(N_ATTN_GROUPS)]
    m = jnp.maximum(jnp.maximum(ls[0], ls[1]), ls[2])
    es = [jnp.exp(l - m) for l in ls]
    den = es[0] + es[1] + es[2]
    for gi in range(N_ATTN_GROUPS):
        alpha = es[gi] / den
        for hh in range(HEADS_PER_GROUP):
            c = hh * HEAD_DIM
            a = alpha[:, hh:hh + 1]
            mix_ref[:, gi * GROUP_WIDTH + c:gi * GROUP_WIDTH + c + HEAD_DIM] = (
                o_rows[gi * HEADS_PER_GROUP + hh] * a).astype(BF16)
    out_ref[...] = x_ref[...] + jnp.dot(mix_ref[...], w_ref[...], preferred_element_type=F32)


def _attn_out_project(os_, lses, x2d, w_out, seq):
    n, d = x2d.shape
    tm = ROW_TILE
    seq_tiles = seq // tm
    width = N_ATTN_GROUPS * GROUP_WIDTH
    dils = [dil for _, dil in DILATED_PATTERNS]
    pm = lambda i: (i // seq_tiles, i % seq_tiles, 0)
    return pl.pallas_call(
        _attn_out_kernel,
        out_shape=jax.ShapeDtypeStruct((n, d), F32),
        grid=(n // tm,),
        in_specs=[pl.BlockSpec((None, tm // dil, dil * GROUP_WIDTH), pm) for dil in dils]
                 + [pl.BlockSpec((None, tm // dil, dil * LANES), pm) for dil in dils]
                 + [pl.BlockSpec((tm, d), lambda i: (i, 0)),
                    pl.BlockSpec((width, d), lambda i: (0, 0))],
        out_specs=pl.BlockSpec((tm, d), lambda i: (i, 0)),
        scratch_shapes=[pltpu.VMEM((tm, width), BF16),
                        pltpu.VMEM((N_ATTN_GROUPS * HEADS_PER_GROUP, tm, HEAD_DIM), F32),
                        pltpu.VMEM((N_ATTN_GROUPS, tm, LANES), F32)],
        compiler_params=_params("parallel"),
        name="attn_out_project",
    )(*os_, *lses, x2d, w_out)


def _router_kernel(x_ref, g_ref, w_ref, b_ref, sel_ref):
    h = _rms(x_ref[...], g_ref[...]).astype(BF16)
    logits = jnp.dot(h, w_ref[...], preferred_element_type=F32) + b_ref[...]
    lane = lax.broadcasted_iota(jnp.int32, logits.shape, 1)
    lanef = lane.astype(F32)
    big = float(LANES)
    is_grp = lane < N_EXPERT_GROUPS
    coarse = jnp.where(is_grp, logits, -jnp.inf)
    cmax = jnp.max(coarse, axis=-1, keepdims=True)
    g_sel = jnp.min(jnp.where(coarse == cmax, lanef, big), axis=-1, keepdims=True)
    den = jnp.sum(jnp.where(is_grp, jnp.exp(logits - cmax), 0.0), axis=-1, keepdims=True)
    g_gate = 1.0 / den
    lo = N_EXPERT_GROUPS + g_sel * EXPERTS_PER_GROUP
    in_grp = (lanef >= lo) & (lanef < lo + EXPERTS_PER_GROUP)
    fine = jnp.where(in_grp, logits, -jnp.inf)
    v1 = jnp.max(fine, axis=-1, keepdims=True)
    i1 = jnp.min(jnp.where(fine == v1, lanef, big), axis=-1, keepdims=True)
    fine2 = jnp.where(lanef == i1, -jnp.inf, fine)
    v2 = jnp.max(fine2, axis=-1, keepdims=True)
    i2 = jnp.min(jnp.where(fine2 == v2, lanef, big), axis=-1, keepdims=True)
    e2 = jnp.exp(v2 - v1)
    w1 = g_gate * (1.0 / (1.0 + e2))
    w2 = g_gate * (e2 / (1.0 + e2))
    sel = jnp.where(lane == 0, w1, 0.0)
    sel = jnp.where(lane == 1, w2, sel)
    sel = jnp.where(lane == 2, i1 - N_EXPERT_GROUPS, sel)
    sel = jnp.where(lane == 3, i2 - N_EXPERT_GROUPS, sel)
    sel_ref[...] = sel


def _route(x2d, g, w_r, b_r):
    n, d = x2d.shape
    tm = ROW_TILE
    return pl.pallas_call(
        _router_kernel,
        out_shape=jax.ShapeDtypeStruct((n, LANES), F32),
        grid=(n // tm,),
        in_specs=[pl.BlockSpec((tm, d), lambda i: (i, 0)),
                  pl.BlockSpec((1, d), lambda i: (0, 0)),
                  pl.BlockSpec((d, LANES), lambda i: (0, 0)),
                  pl.BlockSpec((1, LANES), lambda i: (0, 0))],
        out_specs=pl.BlockSpec((tm, LANES), lambda i: (i, 0)),
        compiler_params=_params("parallel"),
        name="moe_router",
    )(x2d, g, w_r, b_r)


def _expert_kernel(be_ref, first_ref, ws_ref, ne_ref, nu_ref,
                   src_ref, nxt_ref, dst_ref,
                   x_hbm, g_ref, wg_hbm, wu_hbm, wd_hbm,
                   buf_hbm,
                   xbuf, ybuf, wgb, wub, wdb, gsem, ssem, wsem, *, layer):
    i = pl.program_id(0)
    n_used = nu_ref[0]
    slot = lax.rem(i, 2)
    rows = MOE_BLOCK

    def start_gather(idx_ref, s):
        for r in range(rows):
            pltpu.make_async_copy(x_hbm.at[pl.ds(idx_ref[0, 0, r], 1)], xbuf.at[s, pl.ds(r, 1)],
                                  gsem.at[s]).start()

    def wait_gather(s):
        pltpu.make_async_copy(x_hbm.at[pl.ds(0, rows)], xbuf.at[s], gsem.at[s]).wait()

    def start_scatter(s):
        for r in range(rows):
            pltpu.make_async_copy(ybuf.at[s, pl.ds(r, 1)], buf_hbm.at[pl.ds(dst_ref[0, 0, r], 1)],
                                  ssem.at[s]).start()

    def wait_scatter(s):
        pltpu.make_async_copy(ybuf.at[s], buf_hbm.at[pl.ds(0, rows)], ssem.at[s]).wait()

    def weight_copies(e, s):
        return [pltpu.make_async_copy(hbm.at[layer, e], vmem.at[s], wsem.at[s])
                for hbm, vmem in ((wg_hbm, wgb), (wu_hbm, wub), (wd_hbm, wdb))]

    @pl.when(i == 0)
    def _():
        ybuf[...] = jnp.zeros_like(ybuf)
        start_gather(src_ref, 0)
        for c in weight_copies(be_ref[0], 0):
            c.start()

    def compute_step(slot):
        wait_gather(slot)

        @pl.when(i >= 1)
        def _():
            wait_scatter(slot)

        @pl.when(first_ref[i] == 1)
        def _():
            s = ws_ref[i]
            for c in weight_copies(be_ref[i], s):
                c.wait()

            @pl.when(ne_ref[i] >= 0)
            def _():
                for c in weight_copies(ne_ref[i], 1 - s):
                    c.start()

        start_scatter(1 - slot)
        ws = ws_ref[i]
        h = _rms(xbuf[slot], g_ref[...]).astype(BF16)
        gate = jnp.dot(h, wgb[ws].astype(BF16), preferred_element_type=F32)
        up = jnp.dot(h, wub[ws].astype(BF16), preferred_element_type=F32)
        act = (gate * (1.0 / (1.0 + jnp.exp(-gate))) * up).astype(BF16)
        ybuf[slot] = jnp.dot(act, wdb[ws].astype(BF16), preferred_element_type=F32)
        start_gather(nxt_ref, 1 - slot)

    for parity in range(2):
        pl.when((i < n_used) & (slot == parity))(functools.partial(compute_step, parity))

    @pl.when(i == n_used)
    def _():
        wait_scatter(slot)
        start_scatter(1 - slot)
        wait_scatter(1 - slot)
        wait_gather(slot)


def _expert_mlp(x2d, g, w_gate, w_up, w_down, layer, tables, slot_src, slot_dst):
    n, d = x2d.shape
    d_exp = w_gate.shape[-1]
    rows = MOE_BLOCK
    n_blocks = slot_src.shape[0]
    src3 = slot_src.reshape(n_blocks, 1, rows)
    dst3 = slot_dst.reshape(n_blocks + 1, 1, rows)
    smem = functools.partial(pl.BlockSpec, (1, 1, rows), memory_space=pltpu.SMEM)
    grid_spec = pltpu.PrefetchScalarGridSpec(
        num_scalar_prefetch=len(tables),
        grid=(n_blocks + 1,),
        in_specs=[
            smem(lambda i, *_: (jnp.minimum(i, n_blocks - 1), 0, 0)),
            smem(lambda i, *_: (jnp.minimum(i + 1, n_blocks - 1), 0, 0)),
            smem(lambda i, *_: (i, 0, 0)),
            pl.BlockSpec(memory_space=pl.ANY),
            pl.BlockSpec((1, d), lambda i, *_: (0, 0)),
            pl.BlockSpec(memory_space=pl.ANY),
            pl.BlockSpec(memory_space=pl.ANY),
            pl.BlockSpec(memory_space=pl.ANY),
        ],
        out_specs=pl.BlockSpec(memory_space=pl.ANY),
        scratch_shapes=[pltpu.VMEM((2, rows, d), F32), pltpu.VMEM((2, rows, d), F32),
                        pltpu.VMEM((2, d, d_exp), F32), pltpu.VMEM((2, d, d_exp), F32),
                        pltpu.VMEM((2, d_exp, d), F32),
                        pltpu.SemaphoreType.DMA((2,)), pltpu.SemaphoreType.DMA((2,)),
                        pltpu.SemaphoreType.DMA((2,))],
    )
    return pl.pallas_call(
        functools.partial(_expert_kernel, layer=layer),
        out_shape=jax.ShapeDtypeStruct((TOP_K * n + rows, d), F32),
        grid_spec=grid_spec,
        compiler_params=_params("arbitrary"),
        name="moe_expert_mlp",
    )(*tables, src3, src3, dst3, x2d, g, w_gate, w_up, w_down)


def _dispatch_plan(sel, n):
    n_assign = n * TOP_K
    n_blocks = -(-n_assign // MOE_BLOCK) + N_EXPERTS
    e_flat = sel[:, TOP_K:2 * TOP_K].astype(jnp.int32).reshape(-1)
    experts = jnp.arange(N_EXPERTS, dtype=jnp.int32)
    counts = jnp.sum((e_flat[:, None] == experts[None, :]).astype(jnp.int32), axis=0)
    padded = ((counts + MOE_BLOCK - 1) // MOE_BLOCK) * MOE_BLOCK
    pend = jnp.cumsum(padded)
    pstart = pend - padded
    start = jnp.cumsum(counts) - counts
    order = jnp.argsort(e_flat, stable=True).astype(jnp.int32)
    n_used = pend[-1] // MOE_BLOCK
    blk = jnp.arange(n_blocks, dtype=jnp.int32)
    used = blk < n_used

    def per_block(onehot, table):
        return jnp.sum(jnp.where(onehot, table[None, :], 0), axis=1)

    block_e = jnp.minimum(jnp.sum((pend[None, :] <= (blk * MOE_BLOCK)[:, None]).astype(jnp.int32), axis=1),
                          N_EXPERTS - 1)
    last_e = jnp.sum(jnp.where(blk == n_used - 1, block_e, 0))
    block_e = jnp.where(used, block_e, last_e)
    onehot = block_e[:, None] == experts[None, :]
    first_row = blk * MOE_BLOCK - per_block(onehot, pstart)
    n_valid = jnp.where(used, jnp.clip(per_block(onehot, counts) - first_row, 0, MOE_BLOCK), 0)
    nonempty = counts > 0
    w_slot = lax.rem(jnp.cumsum(nonempty.astype(jnp.int32)) - 1, 2)
    later = nonempty[None, :] & (experts[None, :] > experts[:, None])
    next_e = jnp.min(jnp.where(later, experts[None, :], N_EXPERTS), axis=1)
    next_e = jnp.where(next_e == N_EXPERTS, -1, next_e)
    pad1 = lambda v, fill: jnp.concatenate([v, jnp.full((1,), fill, v.dtype)]).astype(jnp.int32)
    tables = (pad1(block_e, 0), pad1((used & (first_row == 0)).astype(jnp.int32), 0),
              pad1(per_block(onehot, w_slot), 0), pad1(per_block(onehot, next_e), -1),
              n_used.reshape(1).astype(jnp.int32))
    in_blk = jnp.arange(MOE_BLOCK, dtype=jnp.int32)[None, :]
    sorted_pos = (per_block(onehot, start) + first_row)[:, None] + in_blk
    a = order[jnp.clip(sorted_pos, 0, n_assign - 1)]
    valid = in_blk < n_valid[:, None]
    tok = a // TOP_K
    tail = jnp.broadcast_to(TOP_K * n + in_blk, a.shape)
    slot_src = jnp.where(valid, tok, 0).astype(jnp.int32)
    slot_dst = jnp.where(valid, (a % TOP_K) * n + tok, tail)
    slot_dst = jnp.concatenate([tail[:1], slot_dst]).astype(jnp.int32)
    return tables, slot_src, slot_dst


def _combine_kernel(x_ref, sel_ref, y0_ref, y1_ref, o_ref):
    sel = sel_ref[...]
    o_ref[...] = x_ref[...] + (y0_ref[...] * sel[:, 0:1] + y1_ref[...] * sel[:, 1:2])


def _combine(x2d, sel, buf):
    n, d = x2d.shape
    tm = ROW_TILE
    tiles = n // tm
    return pl.pallas_call(
        _combine_kernel,
        out_shape=jax.ShapeDtypeStruct((n, d), F32),
        grid=(tiles,),
        in_specs=[pl.BlockSpec((tm, d), lambda i: (i, 0)),
                  pl.BlockSpec((tm, LANES), lambda i: (i, 0)),
                  pl.BlockSpec((tm, d), lambda i: (i, 0)),
                  pl.BlockSpec((tm, d), lambda i: (i + tiles, 0))],
        out_specs=pl.BlockSpec((tm, d), lambda i: (i, 0)),
        compiler_params=_params("parallel"),
        name="moe_combine",
    )(x2d, sel, buf, buf)


def _moe(x2d, g, w_rg, b_rg, w_re, b_re, w_gate, w_up, w_down, layer):
    n, d = x2d.shape
    pad = LANES - N_EXPERT_GROUPS - N_EXPERTS
    w_r = jnp.concatenate([w_rg, w_re, jnp.zeros((d, pad), F32)], axis=1).astype(BF16)
    b_r = jnp.concatenate([b_rg, b_re, jnp.zeros((pad,), F32)]).reshape(1, LANES)
    g2 = g.reshape(1, d)
    sel = _route(x2d, g2, w_r, b_r)
    tables, slot_src, slot_dst = _dispatch_plan(sel, n)
    buf = _expert_mlp(x2d, g2, w_gate, w_up, w_down, layer, tables, slot_src, slot_dst)
    return _combine(x2d, sel, buf)


def _fourier_in_kernel(x_ref, g_ref, w_ref, cc_ref, sc_ref, a_ref, b_ref):
    h = _rms(x_ref[...], g_ref[...]).astype(BF16)
    u = jnp.dot(h, w_ref[...], preferred_element_type=F32).astype(BF16)
    gd = cc_ref.shape[0]
    for gi in range(FOURIER_GROUPS):
        ug = u[:, gi * gd:(gi + 1) * gd]
        a_ref[:, gi * gd:(gi + 1) * gd] = jnp.dot(ug, cc_ref[...], preferred_element_type=F32).astype(BF16)
        b_ref[:, gi * gd:(gi + 1) * gd] = jnp.dot(ug, sc_ref[...], preferred_element_type=F32).astype(BF16)


def _fourier_in(x2d, g, w_in, cc, sc):
    n, d = x2d.shape
    tm = ROW_TILE
    gd = d // FOURIER_GROUPS
    out = jax.ShapeDtypeStruct((n, d), BF16)
    return pl.pallas_call(
        _fourier_in_kernel,
        out_shape=(out, out),
        grid=(n // tm,),
        in_specs=[pl.BlockSpec((tm, d), lambda i: (i, 0)),
                  pl.BlockSpec((1, d), lambda i: (0, 0)),
                  pl.BlockSpec((d, d), lambda i: (0, 0)),
                  pl.BlockSpec((gd, gd), lambda i: (0, 0)),
                  pl.BlockSpec((gd, gd), lambda i: (0, 0))],
        out_specs=(pl.BlockSpec((tm, d), lambda i: (i, 0)), pl.BlockSpec((tm, d), lambda i: (i, 0))),
        compiler_params=_params("parallel"),
        name="fourier_in",
    )(x2d, g, w_in, cc, sc)


def _seq_dft_kernel(cs_ref, ss_ref, a_ref, b_ref, f_ref):
    f = jnp.dot(cs_ref[...], a_ref[...], preferred_element_type=F32)
    f = f + jnp.dot(ss_ref[...], b_ref[...], preferred_element_type=F32)
    f_ref[...] = f.astype(BF16)


def _seq_dft(cs, ss, a3, b3):
    batch, seq, d = a3.shape
    tm = min(seq, 1024)
    tn = min(d, 1024)
    return pl.pallas_call(
        _seq_dft_kernel,
        out_shape=jax.ShapeDtypeStruct((batch, seq, d), BF16),
        grid=(batch, d // tn, seq // tm),
        in_specs=[pl.BlockSpec((tm, seq), lambda b, j, i: (i, 0)),
                  pl.BlockSpec((tm, seq), lambda b, j, i: (i, 0)),
                  pl.BlockSpec((None, seq, tn), lambda b, j, i: (b, 0, j)),
                  pl.BlockSpec((None, seq, tn), lambda b, j, i: (b, 0, j))],
        out_specs=pl.BlockSpec((None, tm, tn), lambda b, j, i: (b, i, j)),
        compiler_params=_params("parallel", "parallel", "arbitrary"),
        name="seq_dft",
    )(cs, ss, a3, b3)


def _proj_residual_kernel(f_ref, w_ref, x_ref, o_ref):
    o_ref[...] = x_ref[...] + jnp.dot(f_ref[...], w_ref[...], preferred_element_type=F32)


def _proj_residual(f2d, w, x2d):
    n, d = x2d.shape
    k = f2d.shape[1]
    tm = ROW_TILE
    return pl.pallas_call(
        _proj_residual_kernel,
        out_shape=jax.ShapeDtypeStruct((n, d), F32),
        grid=(n // tm,),
        in_specs=[pl.BlockSpec((tm, k), lambda i: (i, 0)),
                  pl.BlockSpec((k, d), lambda i: (0, 0)),
                  pl.BlockSpec((tm, d), lambda i: (i, 0))],
        out_specs=pl.BlockSpec((tm, d), lambda i: (i, 0)),
        compiler_params=_params("parallel"),
        name="proj_residual",
    )(f2d, w, x2d)


def _rope_tables(seq):
    half = ROT_DIM // 2
    inv_freq = ROPE_THETA ** (-np.arange(0, ROT_DIM, 2, dtype=np.float64) / ROT_DIM)
    ang = np.arange(seq, dtype=np.float64)[:, None] * inv_freq[None, :]
    cos_t = np.ones((seq, LANES))
    sa = np.zeros((seq, LANES))
    sb = np.zeros((seq, LANES))
    cos_t[:, :half] = np.cos(ang)
    cos_t[:, half:ROT_DIM] = np.cos(ang)
    sa[:, half:ROT_DIM] = np.sin(ang)
    sb[:, :half] = -np.sin(ang)
    return [jnp.asarray(t, F32) for t in (cos_t, sa, sb)]


def _dft_tables(n):
    jk = (np.arange(n, dtype=np.int64)[:, None] * np.arange(n, dtype=np.int64)[None, :]) % n
    ang = 2.0 * np.pi * jk.astype(np.float64) / n
    scale = n ** -0.5
    return np.cos(ang) * scale, np.sin(ang) * scale


def _attention_mixer(x2d, batch, seq, norm_g, w_qkv, q_g, k_g, w_out):
    n, d = x2d.shape
    width = N_ATTN_GROUPS * GROUP_WIDTH
    gw = GROUP_WIDTH
    w3 = jnp.stack([jnp.concatenate([w_qkv[:, t * width + gi * gw:t * width + (gi + 1) * gw]
                                     for t in range(3)], axis=1)
                    for gi in range(N_ATTN_GROUPS)]).astype(BF16)
    qg = (q_g * HEAD_DIM ** -0.5).reshape(1, HEAD_DIM)
    kg = k_g.reshape(1, HEAD_DIM)
    cos_t, sa_t, sb_t = _rope_tables(seq)
    qkv = _qkv_project(x2d, norm_g.reshape(1, d), w3, qg, kg, cos_t, sa_t, sb_t, batch, seq)
    outs = [_attention_group(qkv[gi], gi, batch, seq) for gi in range(N_ATTN_GROUPS)]
    return _attn_out_project([o for o, _ in outs], [l for _, l in outs], x2d, w_out.astype(BF16), seq)


def _fourier_mixer(x2d, batch, seq, norm_g, w_in, w_out):
    n, d = x2d.shape
    gd = d // FOURIER_GROUPS
    cc, sc = _dft_tables(gd)
    cs, ss = _dft_tables(seq)
    cc, sc = jnp.asarray(cc, F32).astype(BF16), jnp.asarray(sc, F32).astype(BF16)
    cs, nss = jnp.asarray(cs, F32).astype(BF16), jnp.asarray(-ss, F32).astype(BF16)
    a, b = _fourier_in(x2d, norm_g.reshape(1, d), w_in.astype(BF16), cc, sc)
    f = _seq_dft(cs, nss, a.reshape(batch, seq, d), b.reshape(batch, seq, d))
    return _proj_residual(f.reshape(n, d), w_out.astype(BF16), x2d)


def kernel(x, attn_norm_g, w_qkv, q_norm_g, k_norm_g, w_attn_out, fourier_norm_g, w_fourier_in, w_fourier_out, moe_norm_g, w_router_group, b_router_group, w_router_expert, b_router_expert, w_expert_gate, w_expert_up, w_expert_down):
    batch, seq, d = x.shape
    depth = moe_norm_g.shape[0]
    h = x.reshape(batch * seq, d)
    for i in range(depth):
        j = i // 2
        if i % 2 == 0:
            h = _attention_mixer(h, batch, seq, attn_norm_g[j], w_qkv[j], q_norm_g[j], k_norm_g[j], w_attn_out[j])
        else:
            h = _fourier_mixer(h, batch, seq, fourier_norm_g[j], w_fourier_in[j], w_fourier_out[j])
        h = _moe(h, moe_norm_g[i], w_router_group[i], b_router_group[i], w_router_expert[i],
                 b_router_expert[i], w_expert_gate, w_expert_up, w_expert_down, i)
    return h.reshape(batch, seq, d)
```

```python
import functools

import numpy as np
import jax
import jax.numpy as jnp
from jax import lax
from jax.experimental import pallas as pl
from jax.experimental.pallas import tpu as pltpu

F32 = jnp.float32
BF16 = jnp.bfloat16

HEAD_DIM = 128
HEADS_PER_GROUP = 4
DILATED_PATTERNS = ((128, 1), (512, 4), (2048, 16))
N_ATTN_GROUPS = len(DILATED_PATTERNS)
GROUP_WIDTH = HEADS_PER_GROUP * HEAD_DIM
ROT_DIM = HEAD_DIM // 4
ROPE_THETA = 500000.0
NEG_INF = -1e30
FOURIER_GROUPS = 8
N_EXPERT_GROUPS = 8
EXPERTS_PER_GROUP = 8
N_EXPERTS = N_EXPERT_GROUPS * EXPERTS_PER_GROUP
TOP_K = 2
MOE_BLOCK = 128
GATHER_SLOTS = 3
EPS = 1e-6

LANES = 128
ATTN_Q_BLOCK = 128
ATTN_TILES_PER_ROUND = 2
ROW_TILE = 512
VMEM_LIMIT = 48 * 1024 * 1024


def _params(*sem):
    return pltpu.CompilerParams(dimension_semantics=sem, vmem_limit_bytes=VMEM_LIMIT)


def _rms(x, g):
    ms = jnp.mean(x * x, axis=-1, keepdims=True)
    return x * lax.rsqrt(ms + EPS) * g


def _qkv_kernel(x_ref, g_ref, w_ref, qg_ref, kg_ref, cos_ref, sa_ref, sb_ref, *rest):
    o_refs, (h_ref, r_ref) = rest[:N_ATTN_GROUPS], rest[N_ATTN_GROUPS:]
    gi_now = pl.program_id(1)

    @pl.when(gi_now == 0)
    def _():
        h_ref[...] = _rms(x_ref[...], g_ref[...]).astype(BF16)

    r = jnp.dot(h_ref[...], w_ref[...], preferred_element_type=F32)
    cos, sa, sb = cos_ref[...], sa_ref[...], sb_ref[...]
    for j in range(2 * HEADS_PER_GROUP):
        t = r[:, j * HEAD_DIM:(j + 1) * HEAD_DIM]
        t = _rms(t, qg_ref[...] if j < HEADS_PER_GROUP else kg_ref[...])
        t = t * cos + pltpu.roll(t, ROT_DIM // 2, 1) * sa + pltpu.roll(t, HEAD_DIM - ROT_DIM // 2, 1) * sb
        r_ref[j] = t
    for j in range(2 * HEADS_PER_GROUP, 3 * HEADS_PER_GROUP):
        r_ref[j] = r[:, j * HEAD_DIM:(j + 1) * HEAD_DIM]

    chunks, tm, _ = r_ref.shape
    for gi, o_ref in enumerate(o_refs):
        dilation = DILATED_PATTERNS[gi][1]

        @pl.when(gi_now == gi)
        def _(o_ref=o_ref, dilation=dilation):
            for phase in range(dilation):
                for c in range(chunks):
                    rows = r_ref[c, pl.ds(phase, tm // dilation, stride=dilation), :]
                    col = (phase * chunks + c) * LANES
                    o_ref[:, col:col + LANES] = rows.astype(BF16)


def _qkv_project(x2d, g, w3, qg, kg, cos_t, sa_t, sb_t, batch, seq):
    n, d = x2d.shape
    tm = ROW_TILE
    seq_tiles = seq // tm
    gw3 = 3 * GROUP_WIDTH
    tab = pl.BlockSpec((tm, LANES), lambda i, j: (i % seq_tiles, 0))
    dils = [dil for _, dil in DILATED_PATTERNS]
    return pl.pallas_call(
        _qkv_kernel,
        out_shape=[jax.ShapeDtypeStruct((batch, seq // dil, dil * gw3), BF16) for dil in dils],
        grid=(n // tm, N_ATTN_GROUPS),
        in_specs=[
            pl.BlockSpec((tm, d), lambda i, j: (i, 0)),
            pl.BlockSpec((1, d), lambda i, j: (0, 0)),
            pl.BlockSpec((None, d, gw3), lambda i, j: (j, 0, 0)),
            pl.BlockSpec((1, LANES), lambda i, j: (0, 0)),
            pl.BlockSpec((1, LANES), lambda i, j: (0, 0)),
            tab, tab, tab,
        ],
        out_specs=[pl.BlockSpec((None, tm // dil, dil * gw3), lambda i, j: (i // seq_tiles, i % seq_tiles, 0))
                   for dil in dils],
        scratch_shapes=[pltpu.VMEM((tm, d), BF16), pltpu.VMEM((gw3 // LANES, tm, LANES), F32)],
        compiler_params=_params("parallel", "arbitrary"),
        name="qkv_project",
    )(x2d, g, w3, qg, kg, cos_t, sa_t, sb_t)


def _attn_kernel(qkv_ref, o_ref, lse_ref, *, dilation, length, radius):
    gw = GROUP_WIDTH
    bq = ATTN_Q_BLOCK
    win = min(length, bq + 2 * radius)
    lane = lax.broadcasted_iota(jnp.int32, (bq, LANES), 1)
    tiles = [(r, qb) for r in range(dilation) for qb in range(length // bq)]
    for t0 in range(0, len(tiles), ATTN_TILES_PER_ROUND):
        work = []
        for r, qb in tiles[t0:t0 + ATTN_TILES_PER_ROUND]:
            q0 = qb * bq
            k0 = min(max(q0 - radius, 0), length - win)
            jq = q0 + lax.broadcasted_iota(jnp.int32, (bq, win), 0)
            jk = k0 + lax.broadcasted_iota(jnp.int32, (bq, win), 1)
            valid = jnp.abs(jk - jq) <= radius
            for hh in range(HEADS_PER_GROUP):
                c = r * 3 * gw + hh * HEAD_DIM
                q = qkv_ref[q0:q0 + bq, c:c + HEAD_DIM]
                k = qkv_ref[k0:k0 + win, c + gw:c + gw + HEAD_DIM]
                s = lax.dot_general(q, k, (((1,), (1,)), ((), ())), preferred_element_type=F32)
                work.append((r, q0, k0, hh, jnp.where(valid, s, NEG_INF)))
        soft = []
        for r, q0, k0, hh, s in work:
            m = jnp.max(s, axis=-1, keepdims=True)
            p = jnp.exp(s - m)
            l = jnp.sum(p, axis=-1, keepdims=True)
            soft.append((p.astype(BF16), l, m + jnp.log(l)))
        lse_tile = None
        for (r, q0, k0, hh, _), (p, l, lse) in zip(work, soft):
            c = r * 3 * gw + hh * HEAD_DIM
            v = qkv_ref[k0:k0 + win, c + 2 * gw:c + 2 * gw + HEAD_DIM]
            o = jnp.dot(p, v, preferred_element_type=F32) / l
            oc = r * gw + hh * HEAD_DIM
            o_ref[q0:q0 + bq, oc:oc + HEAD_DIM] = o.astype(BF16)
            lse_tile = jnp.where(lane == hh, lse, jnp.zeros((bq, LANES), F32) if hh == 0 else lse_tile)
            if hh == HEADS_PER_GROUP - 1:
                lse_ref[q0:q0 + bq, r * LANES:(r + 1) * LANES] = lse_tile


def _attention_group(qkv_g, gi, batch, seq):
    window, dilation = DILATED_PATTERNS[gi]
    radius = (window // 2) // dilation
    length = seq // dilation
    gw3 = 3 * GROUP_WIDTH
    kern = functools.partial(_attn_kernel, dilation=dilation, length=length, radius=radius)
    return pl.pallas_call(
        kern,
        out_shape=(jax.ShapeDtypeStruct((batch, length, dilation * GROUP_WIDTH), BF16),
                   jax.ShapeDtypeStruct((batch, length, dilation * LANES), F32)),
        grid=(batch,),
        in_specs=[pl.BlockSpec((None, length, dilation * gw3), lambda b: (b, 0, 0))],
        out_specs=(pl.BlockSpec((None, length, dilation * GROUP_WIDTH), lambda b: (b, 0, 0)),
                   pl.BlockSpec((None, length, dilation * LANES), lambda b: (b, 0, 0))),
        compiler_params=_params("parallel"),
        name=f"band_attention_d{dilation}",
    )(qkv_g)


def _attn_out_kernel(o0, o1, o2, l0, l1, l2, x_ref, w_ref, out_ref, mix_ref, o_rows, l_rows):
    tm = out_ref.shape[0]
    for gi, (o_ref, l_ref) in enumerate(((o0, l0), (o1, l1), (o2, l2))):
        dilation = DILATED_PATTERNS[gi][1]
        for phase in range(dilation):
            dst = pl.ds(phase, tm // dilation, stride=dilation)
            for hh in range(HEADS_PER_GROUP):
                col = phase * GROUP_WIDTH + hh * HEAD_DIM
                o_rows[gi * HEADS_PER_GROUP + hh, dst, :] = o_ref[:, col:col + HEAD_DIM].astype(F32)
            l_rows[gi, dst, :] = l_ref[:, phase * LANES:(phase + 1) * LANES]
    ls = [l_rows[gi] for gi in range(N_ATTN_GROUPS)]
    m = jnp.maximum(jnp.maximum(ls[0], ls[1]), ls[2])
    es = [jnp.exp(l - m) for l in ls]
    den = es[0] + es[1] + es[2]
    for gi in range(N_ATTN_GROUPS):
        alpha = es[gi] / den
        for hh in range(HEADS_PER_GROUP):
            c = hh * HEAD_DIM
            a = alpha[:, hh:hh + 1]
            mix_ref[:, gi * GROUP_WIDTH + c:gi * GROUP_WIDTH + c + HEAD_DIM] = (
                o_rows[gi * HEADS_PER_GROUP + hh] * a).astype(BF16)
    out_ref[...] = x_ref[...] + jnp.dot(mix_ref[...], w_ref[...], preferred_element_type=F32)


def _attn_out_project(os_, lses, x2d, w_out, seq):
    n, d = x2d.shape
    tm = ROW_TILE
    seq_tiles = seq // tm
    width = N_ATTN_GROUPS * GROUP_WIDTH
    dils = [dil for _, dil in DILATED_PATTERNS]
    pm = lambda i: (i // seq_tiles, i % seq_tiles, 0)
    return pl.pallas_call(
        _attn_out_kernel,
        out_shape=jax.ShapeDtypeStruct((n, d), F32),
        grid=(n // tm,),
        in_specs=[pl.BlockSpec((None, tm // dil, dil * GROUP_WIDTH), pm) for dil in dils]
                 + [pl.BlockSpec((None, tm // dil, dil * LANES), pm) for dil in dils]
                 + [pl.BlockSpec((tm, d), lambda i: (i, 0)),
                    pl.BlockSpec((width, d), lambda i: (0, 0))],
        out_specs=pl.BlockSpec((tm, d), lambda i: (i, 0)),
        scratch_shapes=[pltpu.VMEM((tm, width), BF16),
                        pltpu.VMEM((N_ATTN_GROUPS * HEADS_PER_GROUP, tm, HEAD_DIM), F32),
                        pltpu.VMEM((N_ATTN_GROUPS, tm, LANES), F32)],
        compiler_params=_params("parallel"),
        name="attn_out_project",
    )(*os_, *lses, x2d, w_out)


def _router_kernel(x_ref, g_ref, w_ref, b_ref, sel_ref):
    h = _rms(x_ref[...], g_ref[...]).astype(BF16)
    logits = jnp.dot(h, w_ref[...], preferred_element_type=F32) + b_ref[...]
    lane = lax.broadcasted_iota(jnp.int32, logits.shape, 1)
    lanef = lane.astype(F32)
    big = float(LANES)
    is_grp = lane < N_EXPERT_GROUPS
    coarse = jnp.where(is_grp, logits, -jnp.inf)
    cmax = jnp.max(coarse, axis=-1, keepdims=True)
    g_sel = jnp.min(jnp.where(coarse == cmax, lanef, big), axis=-1, keepdims=True)
    den = jnp.sum(jnp.where(is_grp, jnp.exp(logits - cmax), 0.0), axis=-1, keepdims=True)
    g_gate = 1.0 / den
    lo = N_EXPERT_GROUPS + g_sel * EXPERTS_PER_GROUP
    in_grp = (lanef >= lo) & (lanef < lo + EXPERTS_PER_GROUP)
    fine = jnp.where(in_grp, logits, -jnp.inf)
    v1 = jnp.max(fine, axis=-1, keepdims=True)
    i1 = jnp.min(jnp.where(fine == v1, lanef, big), axis=-1, keepdims=True)
    fine2 = jnp.where(lanef == i1, -jnp.inf, fine)
    v2 = jnp.max(fine2, axis=-1, keepdims=True)
    i2 = jnp.min(jnp.where(fine2 == v2, lanef, big), axis=-1, keepdims=True)
    e2 = jnp.exp(v2 - v1)
    w1 = g_gate * (1.0 / (1.0 + e2))
    w2 = g_gate * (e2 / (1.0 + e2))
    sel = jnp.where(lane == 0, w1, 0.0)
    sel = jnp.where(lane == 1, w2, sel)
    sel = jnp.where(lane == 2, i1 - N_EXPERT_GROUPS, sel)
    sel = jnp.where(lane == 3, i2 - N_EXPERT_GROUPS, sel)
    sel_ref[...] = sel


def _route(x2d, g, w_r, b_r):
    n, d = x2d.shape
    tm = ROW_TILE
    return pl.pallas_call(
        _router_kernel,
        out_shape=jax.ShapeDtypeStruct((n, LANES), F32),
        grid=(n // tm,),
        in_specs=[pl.BlockSpec((tm, d), lambda i: (i, 0)),
                  pl.BlockSpec((1, d), lambda i: (0, 0)),
                  pl.BlockSpec((d, LANES), lambda i: (0, 0)),
                  pl.BlockSpec((1, LANES), lambda i: (0, 0))],
        out_specs=pl.BlockSpec((tm, LANES), lambda i: (i, 0)),
        compiler_params=_params("parallel"),
        name="moe_router",
    )(x2d, g, w_r, b_r)


def _expert_kernel(be_ref, first_ref, ws_ref, ne_ref, nu_ref,
                   src_ref, nxt_ref, nx2_ref, dst_ref,
                   x_hbm, g_ref, wg_hbm, wu_hbm, wd_hbm,
                   buf_hbm,
                   xbuf, ybuf, wgb, wub, wdb, gsem, ssem, wsem, *, layer):
    i = pl.program_id(0)
    n_used = nu_ref[0]
    slot = lax.rem(i, 2)
    gslot = lax.rem(i, GATHER_SLOTS)
    rows = MOE_BLOCK

    def start_gather(idx_ref, s):
        for r in range(rows):
            pltpu.make_async_copy(x_hbm.at[pl.ds(idx_ref[0, 0, r], 1)], xbuf.at[s, pl.ds(r, 1)],
                                  gsem.at[s]).start()

    def wait_gather(s):
        pltpu.make_async_copy(x_hbm.at[pl.ds(0, rows)], xbuf.at[s], gsem.at[s]).wait()

    def start_scatter(s):
        for r in range(rows):
            pltpu.make_async_copy(ybuf.at[s, pl.ds(r, 1)], buf_hbm.at[pl.ds(dst_ref[0, 0, r], 1)],
                                  ssem.at[s]).start(priority=1)

    def wait_scatter(s):
        pltpu.make_async_copy(ybuf.at[s], buf_hbm.at[pl.ds(0, rows)], ssem.at[s]).wait()

    def weight_copies(e, s):
        return [pltpu.make_async_copy(hbm.at[layer, e], vmem.at[s], wsem.at[s])
                for hbm, vmem in ((wg_hbm, wgb), (wu_hbm, wub), (wd_hbm, wdb))]

    @pl.when(i == 0)
    def _():
        ybuf[...] = jnp.zeros_like(ybuf)
        start_gather(src_ref, 0)
        start_gather(nxt_ref, 1)
        for c in weight_copies(be_ref[0], 0):
            c.start()

    @pl.when(i < n_used)
    def _():
        wait_gather(gslot)

        @pl.when(i >= 1)
        def _():
            wait_scatter(slot)

        @pl.when(first_ref[i] == 1)
        def _():
            s = ws_ref[i]
            for c in weight_copies(be_ref[i], s):
                c.wait()

            @pl.when(ne_ref[i] >= 0)
            def _():
                for c in weight_copies(ne_ref[i], 1 - s):
                    c.start()

        start_scatter(1 - slot)
        ws = ws_ref[i]
        h = _rms(xbuf[gslot], g_ref[...]).astype(BF16)
        gate = jnp.dot(h, wgb[ws].astype(BF16), preferred_element_type=F32)
        up = jnp.dot(h, wub[ws].astype(BF16), preferred_element_type=F32)
        act = (gate * (1.0 / (1.0 + jnp.exp(-gate))) * up).astype(BF16)
        ybuf[slot] = jnp.dot(act, wdb[ws].astype(BF16), preferred_element_type=F32)
        start_gather(nx2_ref, lax.rem(i + 2, GATHER_SLOTS))

    @pl.when(i == n_used)
    def _():
        wait_scatter(slot)
        start_scatter(1 - slot)
        wait_scatter(1 - slot)
        wait_gather(gslot)
        wait_gather(lax.rem(i + 1, GATHER_SLOTS))


def _expert_mlp(x2d, g, w_gate, w_up, w_down, layer, tables, slot_src, slot_dst):
    n, d = x2d.shape
    d_exp = w_gate.shape[-1]
    rows = MOE_BLOCK
    n_blocks = slot_src.shape[0]
    src3 = slot_src.reshape(n_blocks, 1, rows)
    dst3 = slot_dst.reshape(n_blocks + 1, 1, rows)
    smem = functools.partial(pl.BlockSpec, (1, 1, rows), memory_space=pltpu.SMEM)
    grid_spec = pltpu.PrefetchScalarGridSpec(
        num_scalar_prefetch=len(tables),
        grid=(n_blocks + 1,),
        in_specs=[
            smem(lambda i, *_: (jnp.minimum(i, n_blocks - 1), 0, 0)),
            smem(lambda i, *_: (jnp.minimum(i + 1, n_blocks - 1), 0, 0)),
            smem(lambda i, *_: (jnp.minimum(i + 2, n_blocks - 1), 0, 0)),
            smem(lambda i, *_: (i, 0, 0)),
            pl.BlockSpec(memory_space=pl.ANY),
            pl.BlockSpec((1, d), lambda i, *_: (0, 0)),
            pl.BlockSpec(memory_space=pl.ANY),
            pl.BlockSpec(memory_space=pl.ANY),
            pl.BlockSpec(memory_space=pl.ANY),
        ],
        out_specs=pl.BlockSpec(memory_space=pl.ANY),
        scratch_shapes=[pltpu.VMEM((GATHER_SLOTS, rows, d), F32), pltpu.VMEM((2, rows, d), F32),
                        pltpu.VMEM((2, d, d_exp), F32), pltpu.VMEM((2, d, d_exp), F32),
                        pltpu.VMEM((2, d_exp, d), F32),
                        pltpu.SemaphoreType.DMA((GATHER_SLOTS,)), pltpu.SemaphoreType.DMA((2,)),
                        pltpu.SemaphoreType.DMA((2,))],
    )
    return pl.pallas_call(
        functools.partial(_expert_kernel, layer=layer),
        out_shape=jax.ShapeDtypeStruct((TOP_K * n + rows, d), F32),
        grid_spec=grid_spec,
        compiler_params=_params("arbitrary"),
        name="moe_expert_mlp",
    )(*tables, src3, src3, src3, dst3, x2d, g, w_gate, w_up, w_down)


def _dispatch_plan(sel, n):
    n_assign = n * TOP_K
    n_blocks = -(-n_assign // MOE_BLOCK) + N_EXPERTS
    e_flat = sel[:, TOP_K:2 * TOP_K].astype(jnp.int32).reshape(-1)
    experts = jnp.arange(N_EXPERTS, dtype=jnp.int32)
    counts = jnp.sum((e_flat[:, None] == experts[None, :]).astype(jnp.int32), axis=0)
    padded = ((counts + MOE_BLOCK - 1) // MOE_BLOCK) * MOE_BLOCK
    pend = jnp.cumsum(padded)
    pstart = pend - padded
    start = jnp.cumsum(counts) - counts
    order = jnp.argsort(e_flat, stable=True).astype(jnp.int32)
    n_used = pend[-1] // MOE_BLOCK
    blk = jnp.arange(n_blocks, dtype=jnp.int32)
    used = blk < n_used

    def per_block(onehot, table):
        return jnp.sum(jnp.where(onehot, table[None, :], 0), axis=1)

    block_e = jnp.minimum(jnp.sum((pend[None, :] <= (blk * MOE_BLOCK)[:, None]).astype(jnp.int32), axis=1),
                          N_EXPERTS - 1)
    last_e = jnp.sum(jnp.where(blk == n_used - 1, block_e, 0))
    block_e = jnp.where(used, block_e, last_e)
    onehot = block_e[:, None] == experts[None, :]
    first_row = blk * MOE_BLOCK - per_block(onehot, pstart)
    n_valid = jnp.where(used, jnp.clip(per_block(onehot, counts) - first_row, 0, MOE_BLOCK), 0)
    nonempty = counts > 0
    w_slot = lax.rem(jnp.cumsum(nonempty.astype(jnp.int32)) - 1, 2)
    later = nonempty[None, :] & (experts[None, :] > experts[:, None])
    next_e = jnp.min(jnp.where(later, experts[None, :], N_EXPERTS), axis=1)
    next_e = jnp.where(next_e == N_EXPERTS, -1, next_e)
    pad1 = lambda v, fill: jnp.concatenate([v, jnp.full((1,), fill, v.dtype)]).astype(jnp.int32)
    tables = (pad1(block_e, 0), pad1((used & (first_row == 0)).astype(jnp.int32), 0),
              pad1(per_block(onehot, w_slot), 0), pad1(per_block(onehot, next_e), -1),
              n_used.reshape(1).astype(jnp.int32))
    in_blk = jnp.arange(MOE_BLOCK, dtype=jnp.int32)[None, :]
    sorted_pos = (per_block(onehot, start) + first_row)[:, None] + in_blk
    a = order[jnp.clip(sorted_pos, 0, n_assign - 1)]
    valid = in_blk < n_valid[:, None]
    tok = a // TOP_K
    tail = jnp.broadcast_to(TOP_K * n + in_blk, a.shape)
    slot_src = jnp.where(valid, tok, 0).astype(jnp.int32)
    slot_dst = jnp.where(valid, (a % TOP_K) * n + tok, tail)
    slot_dst = jnp.concatenate([tail[:1], slot_dst]).astype(jnp.int32)
    return tables, slot_src, slot_dst


def _combine_kernel(x_ref, sel_ref, y0_ref, y1_ref, o_ref):
    sel = sel_ref[...]
    o_ref[...] = x_ref[...] + (y0_ref[...] * sel[:, 0:1] + y1_ref[...] * sel[:, 1:2])


def _combine(x2d, sel, buf):
    n, d = x2d.shape
    tm = ROW_TILE
    tiles = n // tm
    return pl.pallas_call(
        _combine_kernel,
        out_shape=jax.ShapeDtypeStruct((n, d), F32),
        grid=(tiles,),
        in_specs=[pl.BlockSpec((tm, d), lambda i: (i, 0)),
                  pl.BlockSpec((tm, LANES), lambda i: (i, 0)),
                  pl.BlockSpec((tm, d), lambda i: (i, 0)),
                  pl.BlockSpec((tm, d), lambda i: (i + tiles, 0))],
        out_specs=pl.BlockSpec((tm, d), lambda i: (i, 0)),
        compiler_params=_params("parallel"),
        name="moe_combine",
    )(x2d, sel, buf, buf)


def _moe(x2d, g, w_rg, b_rg, w_re, b_re, w_gate, w_up, w_down, layer):
    n, d = x2d.shape
    pad = LANES - N_EXPERT_GROUPS - N_EXPERTS
    w_r = jnp.concatenate([w_rg, w_re, jnp.zeros((d, pad), F32)], axis=1).astype(BF16)
    b_r = jnp.concatenate([b_rg, b_re, jnp.zeros((pad,), F32)]).reshape(1, LANES)
    g2 = g.reshape(1, d)
    sel = _route(x2d, g2, w_r, b_r)
    tables, slot_src, slot_dst = _dispatch_plan(sel, n)
    buf = _expert_mlp(x2d, g2, w_gate, w_up, w_down, layer, tables, slot_src, slot_dst)
    return _combine(x2d, sel, buf)


def _fourier_in_kernel(x_ref, g_ref, w_ref, cc_ref, sc_ref, a_ref, b_ref):
    h = _rms(x_ref[...], g_ref[...]).astype(BF16)
    u = jnp.dot(h, w_ref[...], preferred_element_type=F32).astype(BF16)
    gd = cc_ref.shape[0]
    for gi in range(FOURIER_GROUPS):
        ug = u[:, gi * gd:(gi + 1) * gd]
        a_ref[:, gi * gd:(gi + 1) * gd] = jnp.dot(ug, cc_ref[...], preferred_element_type=F32).astype(BF16)
        b_ref[:, gi * gd:(gi + 1) * gd] = jnp.dot(ug, sc_ref[...], preferred_element_type=F32).astype(BF16)


def _fourier_in(x2d, g, w_in, cc, sc):
    n, d = x2d.shape
    tm = ROW_TILE
    gd = d // FOURIER_GROUPS
    out = jax.ShapeDtypeStruct((n, d), BF16)
    return pl.pallas_call(
        _fourier_in_kernel,
        out_shape=(out, out),
        grid=(n // tm,),
        in_specs=[pl.BlockSpec((tm, d), lambda i: (i, 0)),
                  pl.BlockSpec((1, d), lambda i: (0, 0)),
                  pl.BlockSpec((d, d), lambda i: (0, 0)),
                  pl.BlockSpec((gd, gd), lambda i: (0, 0)),
                  pl.BlockSpec((gd, gd), lambda i: (0, 0))],
        out_specs=(pl.BlockSpec((tm, d), lambda i: (i, 0)), pl.BlockSpec((tm, d), lambda i: (i, 0))),
        compiler_params=_params("parallel"),
        name="fourier_in",
    )(x2d, g, w_in, cc, sc)


def _seq_dft_kernel(cs_ref, ss_ref, a_ref, b_ref, f_ref):
    f = jnp.dot(cs_ref[...], a_ref[...], preferred_element_type=F32)
    f = f + jnp.dot(ss_ref[...], b_ref[...], preferred_element_type=F32)
    f_ref[...] = f.astype(BF16)


def _seq_dft(cs, ss, a3, b3):
    batch, seq, d = a3.shape
    tm = min(seq, 1024)
    tn = min(d, 1024)
    return pl.pallas_call(
        _seq_dft_kernel,
        out_shape=jax.ShapeDtypeStruct((batch, seq, d), BF16),
        grid=(batch, d // tn, seq // tm),
        in_specs=[pl.BlockSpec((tm, seq), lambda b, j, i: (i, 0)),
                  pl.BlockSpec((tm, seq), lambda b, j, i: (i, 0)),
                  pl.BlockSpec((None, seq, tn), lambda b, j, i: (b, 0, j)),
                  pl.BlockSpec((None, seq, tn), lambda b, j, i: (b, 0, j))],
        out_specs=pl.BlockSpec((None, tm, tn), lambda b, j, i: (b, i, j)),
        compiler_params=_params("parallel", "parallel", "arbitrary"),
        name="seq_dft",
    )(cs, ss, a3, b3)


def _proj_residual_kernel(f_ref, w_ref, x_ref, o_ref):
    o_ref[...] = x_ref[...] + jnp.dot(f_ref[...], w_ref[...], preferred_element_type=F32)


def _proj_residual(f2d, w, x2d):
    n, d = x2d.shape
    k = f2d.shape[1]
    tm = ROW_TILE
    return pl.pallas_call(
        _proj_residual_kernel,
        out_shape=jax.ShapeDtypeStruct((n, d), F32),
        grid=(n // tm,),
        in_specs=[pl.BlockSpec((tm, k), lambda i: (i, 0)),
                  pl.BlockSpec((k, d), lambda i: (0, 0)),
                  pl.BlockSpec((tm, d), lambda i: (i, 0))],
        out_specs=pl.BlockSpec((tm, d), lambda i: (i, 0)),
        compiler_params=_params("parallel"),
        name="proj_residual",
    )(f2d, w, x2d)


def _rope_tables(seq):
    half = ROT_DIM // 2
    inv_freq = ROPE_THETA ** (-np.arange(0, ROT_DIM, 2, dtype=np.float64) / ROT_DIM)
    ang = np.arange(seq, dtype=np.float64)[:, None] * inv_freq[None, :]
    cos_t = np.ones((seq, LANES))
    sa = np.zeros((seq, LANES))
    sb = np.zeros((seq, LANES))
    cos_t[:, :half] = np.cos(ang)
    cos_t[:, half:ROT_DIM] = np.cos(ang)
    sa[:, half:ROT_DIM] = np.sin(ang)
    sb[:, :half] = -np.sin(ang)
    return [jnp.asarray(t, F32) for t in (cos_t, sa, sb)]


def _dft_tables(n):
    jk = (np.arange(n, dtype=np.int64)[:, None] * np.arange(n, dtype=np.int64)[None, :]) % n
    ang = 2.0 * np.pi * jk.astype(np.float64) / n
    scale = n ** -0.5
    return np.cos(ang) * scale, np.sin(ang) * scale


def _attention_mixer(x2d, batch, seq, norm_g, w_qkv, q_g, k_g, w_out):
    n, d = x2d.shape
    width = N_ATTN_GROUPS * GROUP_WIDTH
    gw = GROUP_WIDTH
    w3 = jnp.stack([jnp.concatenate([w_qkv[:, t * width + gi * gw:t * width + (gi + 1) * gw]
                                     for t in range(3)], axis=1)
                    for gi in range(N_ATTN_GROUPS)]).astype(BF16)
    qg = (q_g * HEAD_DIM ** -0.5).reshape(1, HEAD_DIM)
    kg = k_g.reshape(1, HEAD_DIM)
    cos_t, sa_t, sb_t = _rope_tables(seq)
    qkv = _qkv_project(x2d, norm_g.reshape(1, d), w3, qg, kg, cos_t, sa_t, sb_t, batch, seq)
    outs = [_attention_group(qkv[gi], gi, batch, seq) for gi in range(N_ATTN_GROUPS)]
    return _attn_out_project([o for o, _ in outs], [l for _, l in outs], x2d, w_out.astype(BF16), seq)


def _fourier_mixer(x2d, batch, seq, norm_g, w_in, w_out):
    n, d = x2d.shape
    gd = d // FOURIER_GROUPS
    cc, sc = _dft_tables(gd)
    cs, ss = _dft_tables(seq)
    cc, sc = jnp.asarray(cc, F32).astype(BF16), jnp.asarray(sc, F32).astype(BF16)
    cs, nss = jnp.asarray(cs, F32).astype(BF16), jnp.asarray(-ss, F32).astype(BF16)
    a, b = _fourier_in(x2d, norm_g.reshape(1, d), w_in.astype(BF16), cc, sc)
    f = _seq_dft(cs, nss, a.reshape(batch, seq, d), b.reshape(batch, seq, d))
    return _proj_residual(f.reshape(n, d), w_out.astype(BF16), x2d)


def kernel(x, attn_norm_g, w_qkv, q_norm_g, k_norm_g, w_attn_out, fourier_norm_g, w_fourier_in, w_fourier_out, moe_norm_g, w_router_group, b_router_group, w_router_expert, b_router_expert, w_expert_gate, w_expert_up, w_expert_down):
    batch, seq, d = x.shape
    depth = moe_norm_g.shape[0]
    h = x.reshape(batch * seq, d)
    for i in range(depth):
        j = i // 2
        if i % 2 == 0:
            h = _attention_mixer(h, batch, seq, attn_norm_g[j], w_qkv[j], q_norm_g[j], k_norm_g[j], w_attn_out[j])
        else:
            h = _fourier_mixer(h, batch, seq, fourier_norm_g[j], w_fourier_in[j], w_fourier_out[j])
        h = _moe(h, moe_norm_g[i], w_router_group[i], b_router_group[i], w_router_expert[i],
                 b_router_expert[i], w_expert_gate, w_expert_up, w_expert_down, i)
    return h.reshape(batch, seq, d)
```

```python
import functools

import numpy as np
import jax
import jax.numpy as jnp
from jax import lax
from jax.experimental import pallas as pl
from jax.experimental.pallas import tpu as pltpu

F32 = jnp.float32
BF16 = jnp.bfloat16

HEAD_DIM = 128
HEADS_PER_GROUP = 4
DILATED_PATTERNS = ((128, 1), (512, 4), (2048, 16))
N_ATTN_GROUPS = len(DILATED_PATTERNS)
GROUP_WIDTH = HEADS_PER_GROUP * HEAD_DIM
ROT_DIM = HEAD_DIM // 4
ROPE_THETA = 500000.0
NEG_INF = -1e30
FOURIER_GROUPS = 8
N_EXPERT_GROUPS = 8
EXPERTS_PER_GROUP = 8
N_EXPERTS = N_EXPERT_GROUPS * EXPERTS_PER_GROUP
TOP_K = 2
MOE_BLOCK = 128
PAIR = 2
EPS = 1e-6

LANES = 128
ATTN_Q_BLOCK = 128
ATTN_TILES_PER_ROUND = 2
ROW_TILE = 512
VMEM_LIMIT = 48 * 1024 * 1024
EXPERT_VMEM_LIMIT = 56 * 1024 * 1024


def _params(*sem):
    return pltpu.CompilerParams(dimension_semantics=sem, vmem_limit_bytes=VMEM_LIMIT)


def _rms(x, g):
    ms = jnp.mean(x * x, axis=-1, keepdims=True)
    return x * lax.rsqrt(ms + EPS) * g


def _qkv_kernel(x_ref, g_ref, w_ref, qg_ref, kg_ref, cos_ref, sa_ref, sb_ref, *rest):
    o_refs, (h_ref, r_ref) = rest[:N_ATTN_GROUPS], rest[N_ATTN_GROUPS:]
    gi_now = pl.program_id(1)

    @pl.when(gi_now == 0)
    def _():
        h_ref[...] = _rms(x_ref[...], g_ref[...]).astype(BF16)

    r = jnp.dot(h_ref[...], w_ref[...], preferred_element_type=F32)
    cos, sa, sb = cos_ref[...], sa_ref[...], sb_ref[...]
    for j in range(2 * HEADS_PER_GROUP):
        t = r[:, j * HEAD_DIM:(j + 1) * HEAD_DIM]
        t = _rms(t, qg_ref[...] if j < HEADS_PER_GROUP else kg_ref[...])
        t = t * cos + pltpu.roll(t, ROT_DIM // 2, 1) * sa + pltpu.roll(t, HEAD_DIM - ROT_DIM // 2, 1) * sb
        r_ref[j] = t
    for j in range(2 * HEADS_PER_GROUP, 3 * HEADS_PER_GROUP):
        r_ref[j] = r[:, j * HEAD_DIM:(j + 1) * HEAD_DIM]

    chunks, tm, _ = r_ref.shape
    for gi, o_ref in enumerate(o_refs):
        dilation = DILATED_PATTERNS[gi][1]

        @pl.when(gi_now == gi)
        def _(o_ref=o_ref, dilation=dilation):
            for phase in range(dilation):
                for c in range(chunks):
                    rows = r_ref[c, pl.ds(phase, tm // dilation, stride=dilation), :]
                    col = (phase * chunks + c) * LANES
                    o_ref[:, col:col + LANES] = rows.astype(BF16)


def _qkv_project(x2d, g, w3, qg, kg, cos_t, sa_t, sb_t, batch, seq):
    n, d = x2d.shape
    tm = ROW_TILE
    seq_tiles = seq // tm
    gw3 = 3 * GROUP_WIDTH
    tab = pl.BlockSpec((tm, LANES), lambda i, j: (i % seq_tiles, 0))
    dils = [dil for _, dil in DILATED_PATTERNS]
    return pl.pallas_call(
        _qkv_kernel,
        out_shape=[jax.ShapeDtypeStruct((batch, seq // dil, dil * gw3), BF16) for dil in dils],
        grid=(n // tm, N_ATTN_GROUPS),
        in_specs=[
            pl.BlockSpec((tm, d), lambda i, j: (i, 0)),
            pl.BlockSpec((1, d), lambda i, j: (0, 0)),
            pl.BlockSpec((None, d, gw3), lambda i, j: (j, 0, 0)),
            pl.BlockSpec((1, LANES), lambda i, j: (0, 0)),
            pl.BlockSpec((1, LANES), lambda i, j: (0, 0)),
            tab, tab, tab,
        ],
        out_specs=[pl.BlockSpec((None, tm // dil, dil * gw3), lambda i, j: (i // seq_tiles, i % seq_tiles, 0))
                   for dil in dils],
        scratch_shapes=[pltpu.VMEM((tm, d), BF16), pltpu.VMEM((gw3 // LANES, tm, LANES), F32)],
        compiler_params=_params("parallel", "arbitrary"),
        name="qkv_project",
    )(x2d, g, w3, qg, kg, cos_t, sa_t, sb_t)


def _attn_kernel(qkv_ref, o_ref, lse_ref, *, dilation, length, radius):
    gw = GROUP_WIDTH
    bq = ATTN_Q_BLOCK
    win = min(length, bq + 2 * radius)
    lane = lax.broadcasted_iota(jnp.int32, (bq, LANES), 1)
    tiles = [(r, qb) for r in range(dilation) for qb in range(length // bq)]
    for t0 in range(0, len(tiles), ATTN_TILES_PER_ROUND):
        work = []
        for r, qb in tiles[t0:t0 + ATTN_TILES_PER_ROUND]:
            q0 = qb * bq
            k0 = min(max(q0 - radius, 0), length - win)
            jq = q0 + lax.broadcasted_iota(jnp.int32, (bq, win), 0)
            jk = k0 + lax.broadcasted_iota(jnp.int32, (bq, win), 1)
            valid = jnp.abs(jk - jq) <= radius
            for hh in range(HEADS_PER_GROUP):
                c = r * 3 * gw + hh * HEAD_DIM
                q = qkv_ref[q0:q0 + bq, c:c + HEAD_DIM]
                k = qkv_ref[k0:k0 + win, c + gw:c + gw + HEAD_DIM]
                s = lax.dot_general(q, k, (((1,), (1,)), ((), ())), preferred_element_type=F32)
                work.append((r, q0, k0, hh, jnp.where(valid, s, NEG_INF)))
        soft = []
        for r, q0, k0, hh, s in work:
            m = jnp.max(s, axis=-1, keepdims=True)
            p = jnp.exp(s - m)
            l = jnp.sum(p, axis=-1, keepdims=True)
            soft.append((p.astype(BF16), l, m + jnp.log(l)))
        lse_tile = None
        for (r, q0, k0, hh, _), (p, l, lse) in zip(work, soft):
            c = r * 3 * gw + hh * HEAD_DIM
            v = qkv_ref[k0:k0 + win, c + 2 * gw:c + 2 * gw + HEAD_DIM]
            o = jnp.dot(p, v, preferred_element_type=F32) / l
            oc = r * gw + hh * HEAD_DIM
            o_ref[q0:q0 + bq, oc:oc + HEAD_DIM] = o.astype(BF16)
            lse_tile = jnp.where(lane == hh, lse, jnp.zeros((bq, LANES), F32) if hh == 0 else lse_tile)
            if hh == HEADS_PER_GROUP - 1:
                lse_ref[q0:q0 + bq, r * LANES:(r + 1) * LANES] = lse_tile


def _attention_group(qkv_g, gi, batch, seq):
    window, dilation = DILATED_PATTERNS[gi]
    radius = (window // 2) // dilation
    length = seq // dilation
    gw3 = 3 * GROUP_WIDTH
    kern = functools.partial(_attn_kernel, dilation=dilation, length=length, radius=radius)
    return pl.pallas_call(
        kern,
        out_shape=(jax.ShapeDtypeStruct((batch, length, dilation * GROUP_WIDTH), BF16),
                   jax.ShapeDtypeStruct((batch, length, dilation * LANES), F32)),
        grid=(batch,),
        in_specs=[pl.BlockSpec((None, length, dilation * gw3), lambda b: (b, 0, 0))],
        out_specs=(pl.BlockSpec((None, length, dilation * GROUP_WIDTH), lambda b: (b, 0, 0)),
                   pl.BlockSpec((None, length, dilation * LANES), lambda b: (b, 0, 0))),
        compiler_params=_params("parallel"),
        name=f"band_attention_d{dilation}",
    )(qkv_g)


def _attn_out_kernel(o0, o1, o2, l0, l1, l2, x_ref, w_ref, out_ref, mix_ref, o_rows, l_rows):
    tm = out_ref.shape[0]
    for gi, (o_ref, l_ref) in enumerate(((o0, l0), (o1, l1), (o2, l2))):
        dilation = DILATED_PATTERNS[gi][1]
        for phase in range(dilation):
            dst = pl.ds(phase, tm // dilation, stride=dilation)
            for hh in range(HEADS_PER_GROUP):
                col = phase * GROUP_WIDTH + hh * HEAD_DIM
                o_rows[gi * HEADS_PER_GROUP + hh, dst, :] = o_ref[:, col:col + HEAD_DIM].astype(F32)
            l_rows[gi, dst, :] = l_ref[:, phase * LANES:(phase + 1) * LANES]
    ls = [l_rows[gi] for gi in range(N_ATTN_GROUPS)]
    m = jnp.maximum(jnp.maximum(ls[0], ls[1]), ls[2])
    es = [jnp.exp(l - m) for l in ls]
    den = es[0] + es[1] + es[2]
    for gi in range(N_ATTN_GROUPS):
        alpha = es[gi] / den
        for hh in range(HEADS_PER_GROUP):
            c = hh * HEAD_DIM
            a = alpha[:, hh:hh + 1]
            mix_ref[:, gi * GROUP_WIDTH + c:gi * GROUP_WIDTH + c + HEAD_DIM] = (
                o_rows[gi * HEADS_PER_GROUP + hh] * a).astype(BF16)
    out_ref[...] = x_ref[...] + jnp.dot(mix_ref[...], w_ref[...], preferred_element_type=F32)


def _attn_out_project(os_, lses, x2d, w_out, seq):
    n, d = x2d.shape
    tm = ROW_TILE
    seq_tiles = seq // tm
    width = N_ATTN_GROUPS * GROUP_WIDTH
    dils = [dil for _, dil in DILATED_PATTERNS]
    pm = lambda i: (i // seq_tiles, i % seq_tiles, 0)
    return pl.pallas_call(
        _attn_out_kernel,
        out_shape=jax.ShapeDtypeStruct((n, d), F32),
        grid=(n // tm,),
        in_specs=[pl.BlockSpec((None, tm // dil, dil * GROUP_WIDTH), pm) for dil in dils]
                 + [pl.BlockSpec((None, tm // dil, dil * LANES), pm) for dil in dils]
                 + [pl.BlockSpec((tm, d), lambda i: (i, 0)),
                    pl.BlockSpec((width, d), lambda i: (0, 0))],
        out_specs=pl.BlockSpec((tm, d), lambda i: (i, 0)),
        scratch_shapes=[pltpu.VMEM((tm, width), BF16),
                        pltpu.VMEM((N_ATTN_GROUPS * HEADS_PER_GROUP, tm, HEAD_DIM), F32),
                        pltpu.VMEM((N_ATTN_GROUPS, tm, LANES), F32)],
        compiler_params=_params("parallel"),
        name="attn_out_project",
    )(*os_, *lses, x2d, w_out)


def _router_kernel(x_ref, g_ref, w_ref, b_ref, sel_ref):
    h = _rms(x_ref[...], g_ref[...]).astype(BF16)
    logits = jnp.dot(h, w_ref[...], preferred_element_type=F32) + b_ref[...]
    lane = lax.broadcasted_iota(jnp.int32, logits.shape, 1)
    lanef = lane.astype(F32)
    big = float(LANES)
    is_grp = lane < N_EXPERT_GROUPS
    coarse = jnp.where(is_grp, logits, -jnp.inf)
    cmax = jnp.max(coarse, axis=-1, keepdims=True)
    g_sel = jnp.min(jnp.where(coarse == cmax, lanef, big), axis=-1, keepdims=True)
    den = jnp.sum(jnp.where(is_grp, jnp.exp(logits - cmax), 0.0), axis=-1, keepdims=True)
    g_gate = 1.0 / den
    lo = N_EXPERT_GROUPS + g_sel * EXPERTS_PER_GROUP
    in_grp = (lanef >= lo) & (lanef < lo + EXPERTS_PER_GROUP)
    fine = jnp.where(in_grp, logits, -jnp.inf)
    v1 = jnp.max(fine, axis=-1, keepdims=True)
    i1 = jnp.min(jnp.where(fine == v1, lanef, big), axis=-1, keepdims=True)
    fine2 = jnp.where(lanef == i1, -jnp.inf, fine)
    v2 = jnp.max(fine2, axis=-1, keepdims=True)
    i2 = jnp.min(jnp.where(fine2 == v2, lanef, big), axis=-1, keepdims=True)
    e2 = jnp.exp(v2 - v1)
    w1 = g_gate * (1.0 / (1.0 + e2))
    w2 = g_gate * (e2 / (1.0 + e2))
    sel = jnp.where(lane == 0, w1, 0.0)
    sel = jnp.where(lane == 1, w2, sel)
    sel = jnp.where(lane == 2, i1 - N_EXPERT_GROUPS, sel)
    sel = jnp.where(lane == 3, i2 - N_EXPERT_GROUPS, sel)
    sel_ref[...] = sel


def _route(x2d, g, w_r, b_r):
    n, d = x2d.shape
    tm = ROW_TILE
    return pl.pallas_call(
        _router_kernel,
        out_shape=jax.ShapeDtypeStruct((n, LANES), F32),
        grid=(n // tm,),
        in_specs=[pl.BlockSpec((tm, d), lambda i: (i, 0)),
                  pl.BlockSpec((1, d), lambda i: (0, 0)),
                  pl.BlockSpec((d, LANES), lambda i: (0, 0)),
                  pl.BlockSpec((1, LANES), lambda i: (0, 0))],
        out_specs=pl.BlockSpec((tm, LANES), lambda i: (i, 0)),
        compiler_params=_params("parallel"),
        name="moe_router",
    )(x2d, g, w_r, b_r)


def _expert_kernel(be_ref, first_ref, ws_ref, ne_ref, nu_ref,
                   src_ref, nxt_ref, dst_ref,
                   x_hbm, g_ref, wg_hbm, wu_hbm, wd_hbm,
                   buf_hbm,
                   xbuf, ybuf, wgb, wub, wdb, gsem, ssem, wsem, *, layer):
    j = pl.program_id(0)
    n_steps = nu_ref[0]
    rows = MOE_BLOCK

    def start_gather(idx_ref, half):
        for k in range(PAIR):
            for r in range(rows):
                pltpu.make_async_copy(x_hbm.at[pl.ds(idx_ref[0, k, r], 1)], xbuf.at[half, k, pl.ds(r, 1)],
                                      gsem.at[half, k]).start()

    def wait_gather(half):
        for k in range(PAIR):
            pltpu.make_async_copy(x_hbm.at[pl.ds(0, rows)], xbuf.at[half, k], gsem.at[half, k]).wait()

    def start_scatter(half):
        for k in range(PAIR):
            for r in range(rows):
                pltpu.make_async_copy(ybuf.at[half, k, pl.ds(r, 1)], buf_hbm.at[pl.ds(dst_ref[0, k, r], 1)],
                                      ssem.at[half, k]).start(priority=1)

    def wait_scatter(half):
        for k in range(PAIR):
            pltpu.make_async_copy(ybuf.at[half, k], buf_hbm.at[pl.ds(0, rows)], ssem.at[half, k]).wait()

    def weight_copies(e, s):
        return [pltpu.make_async_copy(hbm.at[layer, e], vmem.at[s], wsem.at[s])
                for hbm, vmem in ((wg_hbm, wgb), (wu_hbm, wub), (wd_hbm, wdb))]

    @pl.when(j == 0)
    def _():
        ybuf[...] = jnp.zeros_like(ybuf)
        start_gather(src_ref, 0)
        for c in weight_copies(be_ref[0], 0):
            c.start()

    def compute_step(half):
        blocks = [PAIR * j + k for k in range(PAIR)]
        wait_gather(half)

        @pl.when(j >= 1)
        def _():
            wait_scatter(half)

        for k, b in enumerate(blocks):
            @pl.when(first_ref[b] == 1)
            def _(k=k, b=b):
                s = ws_ref[b]
                for c in weight_copies(be_ref[b], s):
                    c.wait()
                if k == 0:
                    @pl.when(ne_ref[b] >= 0)
                    def _():
                        for c in weight_copies(ne_ref[b], 1 - s):
                            c.start()

        start_scatter(1 - half)
        for k, b in enumerate(blocks):
            ws = ws_ref[b]
            h = _rms(xbuf[half, k], g_ref[...]).astype(BF16)
            gate = jnp.dot(h, wgb[ws].astype(BF16), preferred_element_type=F32)
            up = jnp.dot(h, wub[ws].astype(BF16), preferred_element_type=F32)
            act = (gate * (1.0 / (1.0 + jnp.exp(-gate))) * up).astype(BF16)
            ybuf[half, k] = jnp.dot(act, wdb[ws].astype(BF16), preferred_element_type=F32)
        start_gather(nxt_ref, 1 - half)

        for k, b in enumerate(blocks[1:], start=1):
            @pl.when((first_ref[b] == 1) & (ne_ref[b] >= 0))
            def _(b=b):
                for c in weight_copies(ne_ref[b], 1 - ws_ref[b]):
                    c.start()

    for half in range(2):
        pl.when((j < n_steps) & (lax.rem(j, 2) == half))(functools.partial(compute_step, half))

    @pl.when(j == n_steps)
    def _():
        half = lax.rem(j, 2)
        wait_scatter(half)
        start_scatter(1 - half)
        wait_scatter(1 - half)
        wait_gather(half)


def _expert_mlp(x2d, g, w_gate, w_up, w_down, layer, tables, slot_src, slot_dst):
    n, d = x2d.shape
    d_exp = w_gate.shape[-1]
    rows = MOE_BLOCK
    assert slot_src.shape[0] % PAIR == 0
    n_steps = slot_src.shape[0] // PAIR
    src3 = slot_src.reshape(n_steps, PAIR, rows)
    dst3 = slot_dst.reshape(n_steps + 1, PAIR, rows)
    smem = functools.partial(pl.BlockSpec, (1, PAIR, rows), memory_space=pltpu.SMEM)
    grid_spec = pltpu.PrefetchScalarGridSpec(
        num_scalar_prefetch=len(tables),
        grid=(n_steps + 1,),
        in_specs=[
            smem(lambda j, *_: (jnp.minimum(j, n_steps - 1), 0, 0)),
            smem(lambda j, *_: (jnp.minimum(j + 1, n_steps - 1), 0, 0)),
            smem(lambda j, *_: (j, 0, 0)),
            pl.BlockSpec(memory_space=pl.ANY),
            pl.BlockSpec((1, d), lambda j, *_: (0, 0)),
            pl.BlockSpec(memory_space=pl.ANY),
            pl.BlockSpec(memory_space=pl.ANY),
            pl.BlockSpec(memory_space=pl.ANY),
        ],
        out_specs=pl.BlockSpec(memory_space=pl.ANY),
        scratch_shapes=[pltpu.VMEM((2, PAIR, rows, d), F32), pltpu.VMEM((2, PAIR, rows, d), F32),
                        pltpu.VMEM((2, d, d_exp), F32), pltpu.VMEM((2, d, d_exp), F32),
                        pltpu.VMEM((2, d_exp, d), F32),
                        pltpu.SemaphoreType.DMA((2, PAIR)), pltpu.SemaphoreType.DMA((2, PAIR)),
                        pltpu.SemaphoreType.DMA((2,))],
    )
    return pl.pallas_call(
        functools.partial(_expert_kernel, layer=layer),
        out_shape=jax.ShapeDtypeStruct((TOP_K * n + PAIR * rows, d), F32),
        grid_spec=grid_spec,
        compiler_params=pltpu.CompilerParams(dimension_semantics=("arbitrary",),
                                             vmem_limit_bytes=EXPERT_VMEM_LIMIT),
        name="moe_expert_mlp",
    )(*tables, src3, src3, dst3, x2d, g, w_gate, w_up, w_down)


def _dispatch_plan(sel, n):
    n_assign = n * TOP_K
    n_blocks = -(-n_assign // MOE_BLOCK) + N_EXPERTS
    e_flat = sel[:, TOP_K:2 * TOP_K].astype(jnp.int32).reshape(-1)
    experts = jnp.arange(N_EXPERTS, dtype=jnp.int32)
    counts = jnp.sum((e_flat[:, None] == experts[None, :]).astype(jnp.int32), axis=0)
    padded = ((counts + MOE_BLOCK - 1) // MOE_BLOCK) * MOE_BLOCK
    pend = jnp.cumsum(padded)
    pstart = pend - padded
    start = jnp.cumsum(counts) - counts
    order = jnp.argsort(e_flat, stable=True).astype(jnp.int32)
    n_used = pend[-1] // MOE_BLOCK
    blk = jnp.arange(n_blocks, dtype=jnp.int32)
    used = blk < n_used

    def per_block(onehot, table):
        return jnp.sum(jnp.where(onehot, table[None, :], 0), axis=1)

    block_e = jnp.minimum(jnp.sum((pend[None, :] <= (blk * MOE_BLOCK)[:, None]).astype(jnp.int32), axis=1),
                          N_EXPERTS - 1)
    last_e = jnp.sum(jnp.where(blk == n_used - 1, block_e, 0))
    block_e = jnp.where(used, block_e, last_e)
    onehot = block_e[:, None] == experts[None, :]
    first_row = blk * MOE_BLOCK - per_block(onehot, pstart)
    n_valid = jnp.where(used, jnp.clip(per_block(onehot, counts) - first_row, 0, MOE_BLOCK), 0)
    nonempty = counts > 0
    w_slot = lax.rem(jnp.cumsum(nonempty.astype(jnp.int32)) - 1, 2)
    later = nonempty[None, :] & (experts[None, :] > experts[:, None])
    next_e = jnp.min(jnp.where(later, experts[None, :], N_EXPERTS), axis=1)
    next_e = jnp.where(next_e == N_EXPERTS, -1, next_e)
    i32 = lambda v: v.astype(jnp.int32)
    n_steps_used = (n_used + PAIR - 1) // PAIR
    tables = (i32(block_e), i32(used & (first_row == 0)), i32(per_block(onehot, w_slot)),
              i32(per_block(onehot, next_e)), i32(n_steps_used.reshape(1)))
    in_blk = jnp.arange(MOE_BLOCK, dtype=jnp.int32)[None, :]
    sorted_pos = (per_block(onehot, start) + first_row)[:, None] + in_blk
    a = order[jnp.clip(sorted_pos, 0, n_assign - 1)]
    valid = in_blk < n_valid[:, None]
    tok = a // TOP_K
    tail = TOP_K * n + lax.rem(blk, PAIR)[:, None] * MOE_BLOCK + in_blk
    slot_src = i32(jnp.where(valid, tok, 0))
    slot_dst = jnp.where(valid, (a % TOP_K) * n + tok, tail)
    slot_dst = i32(jnp.concatenate([tail[:PAIR], slot_dst]))
    return tables, slot_src, slot_dst


def _combined(x_ref, sel_ref, y0_ref, y1_ref):
    sel = sel_ref[...]
    return x_ref[...] + (y0_ref[...] * sel[:, 0:1] + y1_ref[...] * sel[:, 1:2])


def _pending_specs(tm, n, d):
    tiles = n // tm
    return [pl.BlockSpec((tm, d), lambda i: (i, 0)),
            pl.BlockSpec((tm, LANES), lambda i: (i, 0)),
            pl.BlockSpec((tm, d), lambda i: (i, 0)),
            pl.BlockSpec((tm, d), lambda i: (i + tiles, 0))]


def _combine_kernel(x_ref, sel_ref, y0_ref, y1_ref, o_ref):
    o_ref[...] = _combined(x_ref, sel_ref, y0_ref, y1_ref)


def _combine(pending):
    x2d, sel, buf = pending
    n, d = x2d.shape
    tm = ROW_TILE
    return pl.pallas_call(
        _combine_kernel,
        out_shape=jax.ShapeDtypeStruct((n, d), F32),
        grid=(n // tm,),
        in_specs=_pending_specs(tm, n, d),
        out_specs=pl.BlockSpec((tm, d), lambda i: (i, 0)),
        compiler_params=_params("parallel"),
        name="moe_combine",
    )(x2d, sel, buf, buf)


def _moe(x2d, g, w_rg, b_rg, w_re, b_re, w_gate, w_up, w_down, layer):
    n, d = x2d.shape
    pad = LANES - N_EXPERT_GROUPS - N_EXPERTS
    w_r = jnp.concatenate([w_rg, w_re, jnp.zeros((d, pad), F32)], axis=1).astype(BF16)
    b_r = jnp.concatenate([b_rg, b_re, jnp.zeros((pad,), F32)]).reshape(1, LANES)
    g2 = g.reshape(1, d)
    sel = _route(x2d, g2, w_r, b_r)
    tables, slot_src, slot_dst = _dispatch_plan(sel, n)
    buf = _expert_mlp(x2d, g2, w_gate, w_up, w_down, layer, tables, slot_src, slot_dst)
    return x2d, sel, buf


def _fourier_in_kernel(x_ref, sel_ref, y0_ref, y1_ref, g_ref, w_ref, cc_ref, sc_ref, x_out_ref, a_ref, b_ref):
    x = _combined(x_ref, sel_ref, y0_ref, y1_ref)
    x_out_ref[...] = x
    h = _rms(x, g_ref[...]).astype(BF16)
    u = jnp.dot(h, w_ref[...], preferred_element_type=F32).astype(BF16)
    gd = cc_ref.shape[0]
    for gi in range(FOURIER_GROUPS):
        ug = u[:, gi * gd:(gi + 1) * gd]
        a_ref[:, gi * gd:(gi + 1) * gd] = jnp.dot(ug, cc_ref[...], preferred_element_type=F32).astype(BF16)
        b_ref[:, gi * gd:(gi + 1) * gd] = jnp.dot(ug, sc_ref[...], preferred_element_type=F32).astype(BF16)


def _fourier_in(pending, g, w_in, cc, sc):
    x2d, sel, buf = pending
    n, d = x2d.shape
    tm = ROW_TILE // 2
    gd = d // FOURIER_GROUPS
    out = jax.ShapeDtypeStruct((n, d), BF16)
    row = pl.BlockSpec((tm, d), lambda i: (i, 0))
    return pl.pallas_call(
        _fourier_in_kernel,
        out_shape=(jax.ShapeDtypeStruct((n, d), F32), out, out),
        grid=(n // tm,),
        in_specs=_pending_specs(tm, n, d)
                 + [pl.BlockSpec((1, d), lambda i: (0, 0)),
                    pl.BlockSpec((d, d), lambda i: (0, 0)),
                    pl.BlockSpec((gd, gd), lambda i: (0, 0)),
                    pl.BlockSpec((gd, gd), lambda i: (0, 0))],
        out_specs=(row, row, row),
        compiler_params=_params("parallel"),
        name="fourier_in",
    )(x2d, sel, buf, buf, g, w_in, cc, sc)


def _seq_dft_kernel(cs_ref, ss_ref, a_ref, b_ref, f_ref):
    f = jnp.dot(cs_ref[...], a_ref[...], preferred_element_type=F32)
    f = f + jnp.dot(ss_ref[...], b_ref[...], preferred_element_type=F32)
    f_ref[...] = f.astype(BF16)


def _seq_dft(cs, ss, a3, b3):
    batch, seq, d = a3.shape
    tm = min(seq, 1024)
    tn = min(d, 1024)
    return pl.pallas_call(
        _seq_dft_kernel,
        out_shape=jax.ShapeDtypeStruct((batch, seq, d), BF16),
        grid=(batch, d // tn, seq // tm),
        in_specs=[pl.BlockSpec((tm, seq), lambda b, j, i: (i, 0)),
                  pl.BlockSpec((tm, seq), lambda b, j, i: (i, 0)),
                  pl.BlockSpec((None, seq, tn), lambda b, j, i: (b, 0, j)),
                  pl.BlockSpec((None, seq, tn), lambda b, j, i: (b, 0, j))],
        out_specs=pl.BlockSpec((None, tm, tn), lambda b, j, i: (b, i, j)),
        compiler_params=_params("parallel", "parallel", "arbitrary"),
        name="seq_dft",
    )(cs, ss, a3, b3)


def _proj_residual_kernel(f_ref, w_ref, x_ref, o_ref):
    o_ref[...] = x_ref[...] + jnp.dot(f_ref[...], w_ref[...], preferred_element_type=F32)


def _proj_residual(f2d, w, x2d):
    n, d = x2d.shape
    k = f2d.shape[1]
    tm = ROW_TILE
    return pl.pallas_call(
        _proj_residual_kernel,
        out_shape=jax.ShapeDtypeStruct((n, d), F32),
        grid=(n // tm,),
        in_specs=[pl.BlockSpec((tm, k), lambda i: (i, 0)),
                  pl.BlockSpec((k, d), lambda i: (0, 0)),
                  pl.BlockSpec((tm, d), lambda i: (i, 0))],
        out_specs=pl.BlockSpec((tm, d), lambda i: (i, 0)),
        compiler_params=_params("parallel"),
        name="proj_residual",
    )(f2d, w, x2d)


def _rope_tables(seq):
    half = ROT_DIM // 2
    inv_freq = ROPE_THETA ** (-np.arange(0, ROT_DIM, 2, dtype=np.float64) / ROT_DIM)
    ang = np.arange(seq, dtype=np.float64)[:, None] * inv_freq[None, :]
    cos_t = np.ones((seq, LANES))
    sa = np.zeros((seq, LANES))
    sb = np.zeros((seq, LANES))
    cos_t[:, :half] = np.cos(ang)
    cos_t[:, half:ROT_DIM] = np.cos(ang)
    sa[:, half:ROT_DIM] = np.sin(ang)
    sb[:, :half] = -np.sin(ang)
    return [jnp.asarray(t, F32) for t in (cos_t, sa, sb)]


def _dft_tables(n):
    jk = (np.arange(n, dtype=np.int64)[:, None] * np.arange(n, dtype=np.int64)[None, :]) % n
    ang = 2.0 * np.pi * jk.astype(np.float64) / n
    scale = n ** -0.5
    return np.cos(ang) * scale, np.sin(ang) * scale


def _attention_mixer(x2d, batch, seq, norm_g, w_qkv, q_g, k_g, w_out):
    n, d = x2d.shape
    width = N_ATTN_GROUPS * GROUP_WIDTH
    gw = GROUP_WIDTH
    w3 = jnp.stack([jnp.concatenate([w_qkv[:, t * width + gi * gw:t * width + (gi + 1) * gw]
                                     for t in range(3)], axis=1)
                    for gi in range(N_ATTN_GROUPS)]).astype(BF16)
    qg = (q_g * HEAD_DIM ** -0.5).reshape(1, HEAD_DIM)
    kg = k_g.reshape(1, HEAD_DIM)
    cos_t, sa_t, sb_t = _rope_tables(seq)
    qkv = _qkv_project(x2d, norm_g.reshape(1, d), w3, qg, kg, cos_t, sa_t, sb_t, batch, seq)
    outs = [_attention_group(qkv[gi], gi, batch, seq) for gi in range(N_ATTN_GROUPS)]
    return _attn_out_project([o for o, _ in outs], [l for _, l in outs], x2d, w_out.astype(BF16), seq)


def _fourier_mixer(pending, batch, seq, norm_g, w_in, w_out):
    n, d = pending[0].shape
    gd = d // FOURIER_GROUPS
    cc, sc = _dft_tables(gd)
    cs, ss = _dft_tables(seq)
    cc, sc = jnp.asarray(cc, F32).astype(BF16), jnp.asarray(sc, F32).astype(BF16)
    cs, nss = jnp.asarray(cs, F32).astype(BF16), jnp.asarray(-ss, F32).astype(BF16)
    x2d, a, b = _fourier_in(pending, norm_g.reshape(1, d), w_in.astype(BF16), cc, sc)
    f = _seq_dft(cs, nss, a.reshape(batch, seq, d), b.reshape(batch, seq, d))
    return _proj_residual(f.reshape(n, d), w_out.astype(BF16), x2d)


def kernel(x, attn_norm_g, w_qkv, q_norm_g, k_norm_g, w_attn_out, fourier_norm_g, w_fourier_in, w_fourier_out, moe_norm_g, w_router_group, b_router_group, w_router_expert, b_router_expert, w_expert_gate, w_expert_up, w_expert_down):
    batch, seq, d = x.shape
    depth = moe_norm_g.shape[0]
    h = x.reshape(batch * seq, d)
    pending = None
    for i in range(depth):
        j = i // 2
        if i % 2 == 0:
            if pending is not None:
                h = _combine(pending)
            h = _attention_mixer(h, batch, seq, attn_norm_g[j], w_qkv[j], q_norm_g[j], k_norm_g[j], w_attn_out[j])
        else:
            h = _fourier_mixer(pending, batch, seq, fourier_norm_g[j], w_fourier_in[j], w_fourier_out[j])
        pending = _moe(h, moe_norm_g[i], w_router_group[i], b_router_group[i], w_router_expert[i],
                       b_router_expert[i], w_expert_gate, w_expert_up, w_expert_down, i)
    return _combine(pending).reshape(batch, seq, d)
```

```python
import functools

import numpy as np
import jax
import jax.numpy as jnp
from jax import lax
from jax.experimental import pallas as pl
from jax.experimental.pallas import tpu as pltpu

F32 = jnp.float32
BF16 = jnp.bfloat16

HEAD_DIM = 128
HEADS_PER_GROUP = 4
DILATED_PATTERNS = ((128, 1), (512, 4), (2048, 16))
N_ATTN_GROUPS = len(DILATED_PATTERNS)
GROUP_WIDTH = HEADS_PER_GROUP * HEAD_DIM
ROT_DIM = HEAD_DIM // 4
ROPE_THETA = 500000.0
NEG_INF = -1e30
FOURIER_GROUPS = 8
N_EXPERT_GROUPS = 8
EXPERTS_PER_GROUP = 8
N_EXPERTS = N_EXPERT_GROUPS * EXPERTS_PER_GROUP
TOP_K = 2
MOE_BLOCK = 128
GATHER_SLOTS = 3
EPS = 1e-6

LANES = 128
ATTN_Q_BLOCK = 128
ATTN_TILES_PER_ROUND = 2
ROW_TILE = 512
VMEM_LIMIT = 48 * 1024 * 1024


def _params(*sem):
    return pltpu.CompilerParams(dimension_semantics=sem, vmem_limit_bytes=VMEM_LIMIT)


def _rms(x, g):
    ms = jnp.mean(x * x, axis=-1, keepdims=True)
    return x * lax.rsqrt(ms + EPS) * g


def _qkv_kernel(x_ref, g_ref, w_ref, qg_ref, kg_ref, cos_ref, sa_ref, sb_ref, *rest):
    o_refs, (h_ref, r_ref) = rest[:N_ATTN_GROUPS], rest[N_ATTN_GROUPS:]
    gi_now = pl.program_id(1)

    @pl.when(gi_now == 0)
    def _():
        h_ref[...] = _rms(x_ref[...], g_ref[...]).astype(BF16)

    r = jnp.dot(h_ref[...], w_ref[...], preferred_element_type=F32)
    cos, sa, sb = cos_ref[...], sa_ref[...], sb_ref[...]
    for j in range(2 * HEADS_PER_GROUP):
        t = r[:, j * HEAD_DIM:(j + 1) * HEAD_DIM]
        t = _rms(t, qg_ref[...] if j < HEADS_PER_GROUP else kg_ref[...])
        t = t * cos + pltpu.roll(t, ROT_DIM // 2, 1) * sa + pltpu.roll(t, HEAD_DIM - ROT_DIM // 2, 1) * sb
        r_ref[j] = t
    for j in range(2 * HEADS_PER_GROUP, 3 * HEADS_PER_GROUP):
        r_ref[j] = r[:, j * HEAD_DIM:(j + 1) * HEAD_DIM]

    chunks, tm, _ = r_ref.shape
    for gi, o_ref in enumerate(o_refs):
        dilation = DILATED_PATTERNS[gi][1]

        @pl.when(gi_now == gi)
        def _(o_ref=o_ref, dilation=dilation):
            for phase in range(dilation):
                for c in range(chunks):
                    rows = r_ref[c, pl.ds(phase, tm // dilation, stride=dilation), :]
                    col = (phase * chunks + c) * LANES
                    o_ref[:, col:col + LANES] = rows.astype(BF16)


def _qkv_project(x2d, g, w3, qg, kg, cos_t, sa_t, sb_t, batch, seq):
    n, d = x2d.shape
    tm = ROW_TILE
    seq_tiles = seq // tm
    gw3 = 3 * GROUP_WIDTH
    tab = pl.BlockSpec((tm, LANES), lambda i, j: (i % seq_tiles, 0))
    dils = [dil for _, dil in DILATED_PATTERNS]
    return pl.pallas_call(
        _qkv_kernel,
        out_shape=[jax.ShapeDtypeStruct((batch, seq // dil, dil * gw3), BF16) for dil in dils],
        grid=(n // tm, N_ATTN_GROUPS),
        in_specs=[
            pl.BlockSpec((tm, d), lambda i, j: (i, 0)),
            pl.BlockSpec((1, d), lambda i, j: (0, 0)),
            pl.BlockSpec((None, d, gw3), lambda i, j: (j, 0, 0)),
            pl.BlockSpec((1, LANES), lambda i, j: (0, 0)),
            pl.BlockSpec((1, LANES), lambda i, j: (0, 0)),
            tab, tab, tab,
        ],
        out_specs=[pl.BlockSpec((None, tm // dil, dil * gw3), lambda i, j: (i // seq_tiles, i % seq_tiles, 0))
                   for dil in dils],
        scratch_shapes=[pltpu.VMEM((tm, d), BF16), pltpu.VMEM((gw3 // LANES, tm, LANES), F32)],
        compiler_params=_params("parallel", "arbitrary"),
        name="qkv_project",
    )(x2d, g, w3, qg, kg, cos_t, sa_t, sb_t)


def _attn_kernel(qkv_ref, o_ref, lse_ref, *, dilation, length, radius):
    gw = GROUP_WIDTH
    bq = ATTN_Q_BLOCK
    win = min(length, bq + 2 * radius)
    lane = lax.broadcasted_iota(jnp.int32, (bq, LANES), 1)
    tiles = [(r, qb) for r in range(dilation) for qb in range(length // bq)]
    for t0 in range(0, len(tiles), ATTN_TILES_PER_ROUND):
        work = []
        for r, qb in tiles[t0:t0 + ATTN_TILES_PER_ROUND]:
            q0 = qb * bq
            k0 = min(max(q0 - radius, 0), length - win)
            jq = q0 + lax.broadcasted_iota(jnp.int32, (bq, win), 0)
            jk = k0 + lax.broadcasted_iota(jnp.int32, (bq, win), 1)
            valid = jnp.abs(jk - jq) <= radius
            for hh in range(HEADS_PER_GROUP):
                c = r * 3 * gw + hh * HEAD_DIM
                q = qkv_ref[q0:q0 + bq, c:c + HEAD_DIM]
                k = qkv_ref[k0:k0 + win, c + gw:c + gw + HEAD_DIM]
                s = lax.dot_general(q, k, (((1,), (1,)), ((), ())), preferred_element_type=F32)
                work.append((r, q0, k0, hh, jnp.where(valid, s, NEG_INF)))
        soft = []
        for r, q0, k0, hh, s in work:
            m = jnp.max(s, axis=-1, keepdims=True)
            p = jnp.exp(s - m)
            l = jnp.sum(p, axis=-1, keepdims=True)
            soft.append((p.astype(BF16), l, m + jnp.log(l)))
        lse_tile = None
        for (r, q0, k0, hh, _), (p, l, lse) in zip(work, soft):
            c = r * 3 * gw + hh * HEAD_DIM
            v = qkv_ref[k0:k0 + win, c + 2 * gw:c + 2 * gw + HEAD_DIM]
            o = jnp.dot(p, v, preferred_element_type=F32) / l
            oc = r * gw + hh * HEAD_DIM
            o_ref[q0:q0 + bq, oc:oc + HEAD_DIM] = o.astype(BF16)
            lse_tile = jnp.where(lane == hh, lse, jnp.zeros((bq, LANES), F32) if hh == 0 else lse_tile)
            if hh == HEADS_PER_GROUP - 1:
                lse_ref[q0:q0 + bq, r * LANES:(r + 1) * LANES] = lse_tile


def _attention_group(qkv_g, gi, batch, seq):
    window, dilation = DILATED_PATTERNS[gi]
    radius = (window // 2) // dilation
    length = seq // dilation
    gw3 = 3 * GROUP_WIDTH
    kern = functools.partial(_attn_kernel, dilation=dilation, length=length, radius=radius)
    return pl.pallas_call(
        kern,
        out_shape=(jax.ShapeDtypeStruct((batch, length, dilation * GROUP_WIDTH), BF16),
                   jax.ShapeDtypeStruct((batch, length, dilation * LANES), F32)),
        grid=(batch,),
        in_specs=[pl.BlockSpec((None, length, dilation * gw3), lambda b: (b, 0, 0))],
        out_specs=(pl.BlockSpec((None, length, dilation * GROUP_WIDTH), lambda b: (b, 0, 0)),
                   pl.BlockSpec((None, length, dilation * LANES), lambda b: (b, 0, 0))),
        compiler_params=_params("parallel"),
        name=f"band_attention_d{dilation}",
    )(qkv_g)


def _attn_out_kernel(o0, o1, o2, l0, l1, l2, x_ref, w_ref, out_ref, mix_ref, o_rows, l_rows):
    tm = out_ref.shape[0]
    for gi, (o_ref, l_ref) in enumerate(((o0, l0), (o1, l1), (o2, l2))):
        dilation = DILATED_PATTERNS[gi][1]
        for phase in range(dilation):
            dst = pl.ds(phase, tm // dilation, stride=dilation)
            for hh in range(HEADS_PER_GROUP):
                col = phase * GROUP_WIDTH + hh * HEAD_DIM
                o_rows[gi * HEADS_PER_GROUP + hh, dst, :] = o_ref[:, col:col + HEAD_DIM].astype(F32)
            l_rows[gi, dst, :] = l_ref[:, phase * LANES:(phase + 1) * LANES]
    ls = [l_rows[gi] for gi in range(N_ATTN_GROUPS)]
    m = jnp.maximum(jnp.maximum(ls[0], ls[1]), ls[2])
    es = [jnp.exp(l - m) for l in ls]
    den = es[0] + es[1] + es[2]
    for gi in range(N_ATTN_GROUPS):
        alpha = es[gi] / den
        for hh in range(HEADS_PER_GROUP):
            c = hh * HEAD_DIM
            a = alpha[:, hh:hh + 1]
            mix_ref[:, gi * GROUP_WIDTH + c:gi * GROUP_WIDTH + c + HEAD_DIM] = (
                o_rows[gi * HEADS_PER_GROUP + hh] * a).astype(BF16)
    out_ref[...] = x_ref[...] + jnp.dot(mix_ref[...], w_ref[...], preferred_element_type=F32)


def _attn_out_project(os_, lses, x2d, w_out, seq):
    n, d = x2d.shape
    tm = ROW_TILE
    seq_tiles = seq // tm
    width = N_ATTN_GROUPS * GROUP_WIDTH
    dils = [dil for _, dil in DILATED_PATTERNS]
    pm = lambda i: (i // seq_tiles, i % seq_tiles, 0)
    return pl.pallas_call(
        _attn_out_kernel,
        out_shape=jax.ShapeDtypeStruct((n, d), F32),
        grid=(n // tm,),
        in_specs=[pl.BlockSpec((None, tm // dil, dil * GROUP_WIDTH), pm) for dil in dils]
                 + [pl.BlockSpec((None, tm // dil, dil * LANES), pm) for dil in dils]
                 + [pl.BlockSpec((tm, d), lambda i: (i, 0)),
                    pl.BlockSpec((width, d), lambda i: (0, 0))],
        out_specs=pl.BlockSpec((tm, d), lambda i: (i, 0)),
        scratch_shapes=[pltpu.VMEM((tm, width), BF16),
                        pltpu.VMEM((N_ATTN_GROUPS * HEADS_PER_GROUP, tm, HEAD_DIM), F32),
                        pltpu.VMEM((N_ATTN_GROUPS, tm, LANES), F32)],
        compiler_params=_params("parallel"),
        name="attn_out_project",
    )(*os_, *lses, x2d, w_out)


U32 = jnp.uint32
HIGH_HALF = np.uint32(0xFFFF0000)


def _pack_halves(x):
    c = x.shape[1] // 2
    lo = lax.bitcast_convert_type(x[:, :c].astype(BF16).astype(F32), U32)
    hi = lax.bitcast_convert_type(x[:, c:].astype(BF16).astype(F32), U32)
    return (lo >> 16) | (hi & HIGH_HALF)


def _unpack_halves(u):
    lo = lax.bitcast_convert_type(u << 16, F32)
    hi = lax.bitcast_convert_type(u & HIGH_HALF, F32)
    return jnp.concatenate([lo, hi], axis=1)


def _router_kernel(x_ref, g_ref, w_ref, b_ref, sel_ref, hp_ref):
    hf = _rms(x_ref[...], g_ref[...])
    hp_ref[...] = _pack_halves(hf)
    h = hf.astype(BF16)
    logits = jnp.dot(h, w_ref[...], preferred_element_type=F32) + b_ref[...]
    lane = lax.broadcasted_iota(jnp.int32, logits.shape, 1)
    lanef = lane.astype(F32)
    big = float(LANES)
    is_grp = lane < N_EXPERT_GROUPS
    coarse = jnp.where(is_grp, logits, -jnp.inf)
    cmax = jnp.max(coarse, axis=-1, keepdims=True)
    g_sel = jnp.min(jnp.where(coarse == cmax, lanef, big), axis=-1, keepdims=True)
    den = jnp.sum(jnp.where(is_grp, jnp.exp(logits - cmax), 0.0), axis=-1, keepdims=True)
    g_gate = 1.0 / den
    lo = N_EXPERT_GROUPS + g_sel * EXPERTS_PER_GROUP
    in_grp = (lanef >= lo) & (lanef < lo + EXPERTS_PER_GROUP)
    fine = jnp.where(in_grp, logits, -jnp.inf)
    v1 = jnp.max(fine, axis=-1, keepdims=True)
    i1 = jnp.min(jnp.where(fine == v1, lanef, big), axis=-1, keepdims=True)
    fine2 = jnp.where(lanef == i1, -jnp.inf, fine)
    v2 = jnp.max(fine2, axis=-1, keepdims=True)
    i2 = jnp.min(jnp.where(fine2 == v2, lanef, big), axis=-1, keepdims=True)
    e2 = jnp.exp(v2 - v1)
    w1 = g_gate * (1.0 / (1.0 + e2))
    w2 = g_gate * (e2 / (1.0 + e2))
    sel = jnp.where(lane == 0, w1, 0.0)
    sel = jnp.where(lane == 1, w2, sel)
    sel = jnp.where(lane == 2, i1 - N_EXPERT_GROUPS, sel)
    sel = jnp.where(lane == 3, i2 - N_EXPERT_GROUPS, sel)
    sel_ref[...] = sel


def _route(x2d, g, w_r, b_r):
    n, d = x2d.shape
    tm = ROW_TILE
    return pl.pallas_call(
        _router_kernel,
        out_shape=(jax.ShapeDtypeStruct((n, LANES), F32), jax.ShapeDtypeStruct((n, d // 2), U32)),
        grid=(n // tm,),
        in_specs=[pl.BlockSpec((tm, d), lambda i: (i, 0)),
                  pl.BlockSpec((1, d), lambda i: (0, 0)),
                  pl.BlockSpec((d, LANES), lambda i: (0, 0)),
                  pl.BlockSpec((1, LANES), lambda i: (0, 0))],
        out_specs=(pl.BlockSpec((tm, LANES), lambda i: (i, 0)), pl.BlockSpec((tm, d // 2), lambda i: (i, 0))),
        compiler_params=_params("parallel"),
        name="moe_router",
    )(x2d, g, w_r, b_r)


def _expert_kernel(be_ref, first_ref, ws_ref, ne_ref, nu_ref,
                   src_ref, nxt_ref, nx2_ref, dst_ref,
                   h_hbm, wg_hbm, wu_hbm, wd_hbm,
                   buf_hbm,
                   xbuf, ybuf, wgb, wub, wdb, gsem, ssem, wsem, *, layer):
    i = pl.program_id(0)
    n_used = nu_ref[0]
    slot = lax.rem(i, 2)
    gslot = lax.rem(i, GATHER_SLOTS)
    rows = MOE_BLOCK

    def start_gather(idx_ref, s):
        for r in range(rows):
            pltpu.make_async_copy(h_hbm.at[pl.ds(idx_ref[0, 0, r], 1)], xbuf.at[s, pl.ds(r, 1)],
                                  gsem.at[s]).start()

    def wait_gather(s):
        pltpu.make_async_copy(h_hbm.at[pl.ds(0, rows)], xbuf.at[s], gsem.at[s]).wait()

    def start_scatter(s):
        for r in range(rows):
            pltpu.make_async_copy(ybuf.at[s, pl.ds(r, 1)], buf_hbm.at[pl.ds(dst_ref[0, 0, r], 1)],
                                  ssem.at[s]).start(priority=1)

    def wait_scatter(s):
        pltpu.make_async_copy(ybuf.at[s], buf_hbm.at[pl.ds(0, rows)], ssem.at[s]).wait()

    def weight_copies(e, s):
        return [pltpu.make_async_copy(hbm.at[layer, e], vmem.at[s], wsem.at[s])
                for hbm, vmem in ((wg_hbm, wgb), (wu_hbm, wub), (wd_hbm, wdb))]

    @pl.when(i == 0)
    def _():
        ybuf[...] = jnp.zeros_like(ybuf)
        start_gather(src_ref, 0)
        start_gather(nxt_ref, 1)
        for c in weight_copies(be_ref[0], 0):
            c.start()

    @pl.when(i < n_used)
    def _():
        wait_gather(gslot)

        @pl.when(i >= 1)
        def _():
            wait_scatter(slot)

        @pl.when(first_ref[i] == 1)
        def _():
            s = ws_ref[i]
            for c in weight_copies(be_ref[i], s):
                c.wait()

            @pl.when(ne_ref[i] >= 0)
            def _():
                for c in weight_copies(ne_ref[i], 1 - s):
                    c.start()

        start_scatter(1 - slot)
        ws = ws_ref[i]
        h = _unpack_halves(xbuf[gslot]).astype(BF16)
        gate = jnp.dot(h, wgb[ws].astype(BF16), preferred_element_type=F32)
        up = jnp.dot(h, wub[ws].astype(BF16), preferred_element_type=F32)
        act = (gate * (1.0 / (1.0 + jnp.exp(-gate))) * up).astype(BF16)
        ybuf[slot] = _pack_halves(jnp.dot(act, wdb[ws].astype(BF16), preferred_element_type=F32))
        start_gather(nx2_ref, lax.rem(i + 2, GATHER_SLOTS))

    @pl.when(i == n_used)
    def _():
        wait_scatter(slot)
        start_scatter(1 - slot)
        wait_scatter(1 - slot)
        wait_gather(gslot)
        wait_gather(lax.rem(i + 1, GATHER_SLOTS))


def _expert_mlp(hp, w_gate, w_up, w_down, layer, tables, slot_src, slot_dst):
    n, dp = hp.shape
    d_exp = w_gate.shape[-1]
    d = w_gate.shape[-2]
    rows = MOE_BLOCK
    n_blocks = slot_src.shape[0]
    src3 = slot_src.reshape(n_blocks, 1, rows)
    dst3 = slot_dst.reshape(n_blocks + 1, 1, rows)
    smem = functools.partial(pl.BlockSpec, (1, 1, rows), memory_space=pltpu.SMEM)
    grid_spec = pltpu.PrefetchScalarGridSpec(
        num_scalar_prefetch=len(tables),
        grid=(n_blocks + 1,),
        in_specs=[
            smem(lambda i, *_: (jnp.minimum(i, n_blocks - 1), 0, 0)),
            smem(lambda i, *_: (jnp.minimum(i + 1, n_blocks - 1), 0, 0)),
            smem(lambda i, *_: (jnp.minimum(i + 2, n_blocks - 1), 0, 0)),
            smem(lambda i, *_: (i, 0, 0)),
            pl.BlockSpec(memory_space=pl.ANY),
            pl.BlockSpec(memory_space=pl.ANY),
            pl.BlockSpec(memory_space=pl.ANY),
            pl.BlockSpec(memory_space=pl.ANY),
        ],
        out_specs=pl.BlockSpec(memory_space=pl.ANY),
        scratch_shapes=[pltpu.VMEM((GATHER_SLOTS, rows, dp), U32), pltpu.VMEM((2, rows, dp), U32),
                        pltpu.VMEM((2, d, d_exp), F32), pltpu.VMEM((2, d, d_exp), F32),
                        pltpu.VMEM((2, d_exp, d), F32),
                        pltpu.SemaphoreType.DMA((GATHER_SLOTS,)), pltpu.SemaphoreType.DMA((2,)),
                        pltpu.SemaphoreType.DMA((2,))],
    )
    return pl.pallas_call(
        functools.partial(_expert_kernel, layer=layer),
        out_shape=jax.ShapeDtypeStruct((TOP_K * n + rows, dp), U32),
        grid_spec=grid_spec,
        compiler_params=_params("arbitrary"),
        name="moe_expert_mlp",
    )(*tables, src3, src3, src3, dst3, hp, w_gate, w_up, w_down)


def _dispatch_plan(sel, n):
    n_assign = n * TOP_K
    n_blocks = -(-n_assign // MOE_BLOCK) + N_EXPERTS
    e_flat = sel[:, TOP_K:2 * TOP_K].astype(jnp.int32).reshape(-1)
    experts = jnp.arange(N_EXPERTS, dtype=jnp.int32)
    counts = jnp.sum((e_flat[:, None] == experts[None, :]).astype(jnp.int32), axis=0)
    padded = ((counts + MOE_BLOCK - 1) // MOE_BLOCK) * MOE_BLOCK
    pend = jnp.cumsum(padded)
    pstart = pend - padded
    start = jnp.cumsum(counts) - counts
    order = jnp.argsort(e_flat, stable=True).astype(jnp.int32)
    n_used = pend[-1] // MOE_BLOCK
    blk = jnp.arange(n_blocks, dtype=jnp.int32)
    used = blk < n_used

    def per_block(onehot, table):
        return jnp.sum(jnp.where(onehot, table[None, :], 0), axis=1)

    block_e = jnp.minimum(jnp.sum((pend[None, :] <= (blk * MOE_BLOCK)[:, None]).astype(jnp.int32), axis=1),
                          N_EXPERTS - 1)
    last_e = jnp.sum(jnp.where(blk == n_used - 1, block_e, 0))
    block_e = jnp.where(used, block_e, last_e)
    onehot = block_e[:, None] == experts[None, :]
    first_row = blk * MOE_BLOCK - per_block(onehot, pstart)
    n_valid = jnp.where(used, jnp.clip(per_block(onehot, counts) - first_row, 0, MOE_BLOCK), 0)
    nonempty = counts > 0
    w_slot = lax.rem(jnp.cumsum(nonempty.astype(jnp.int32)) - 1, 2)
    later = nonempty[None, :] & (experts[None, :] > experts[:, None])
    next_e = jnp.min(jnp.where(later, experts[None, :], N_EXPERTS), axis=1)
    next_e = jnp.where(next_e == N_EXPERTS, -1, next_e)
    i32 = lambda v: v.astype(jnp.int32)
    pad1 = lambda v, fill: i32(jnp.concatenate([v, jnp.full((1,), fill, v.dtype)]))
    tables = (pad1(block_e, 0), pad1(i32(used & (first_row == 0)), 0),
              pad1(per_block(onehot, w_slot), 0), pad1(per_block(onehot, next_e), -1),
              i32(n_used.reshape(1)))
    in_blk = jnp.arange(MOE_BLOCK, dtype=jnp.int32)[None, :]
    sorted_pos = (per_block(onehot, start) + first_row)[:, None] + in_blk
    a = order[jnp.clip(sorted_pos, 0, n_assign - 1)]
    valid = in_blk < n_valid[:, None]
    tok = a // TOP_K
    tail = jnp.broadcast_to(TOP_K * n + in_blk, a.shape)
    slot_src = i32(jnp.where(valid, tok, 0))
    slot_dst = jnp.where(valid, (a % TOP_K) * n + tok, tail)
    slot_dst = i32(jnp.concatenate([tail[:1], slot_dst]))
    return tables, slot_src, slot_dst


def _combined(x_ref, sel_ref, y0_ref, y1_ref):
    sel = sel_ref[...]
    y0, y1 = _unpack_halves(y0_ref[...]), _unpack_halves(y1_ref[...])
    return x_ref[...] + (y0 * sel[:, 0:1] + y1 * sel[:, 1:2])


def _pending_specs(tm, n, d):
    tiles = n // tm
    return [pl.BlockSpec((tm, d), lambda i: (i, 0)),
            pl.BlockSpec((tm, LANES), lambda i: (i, 0)),
            pl.BlockSpec((tm, d // 2), lambda i: (i, 0)),
            pl.BlockSpec((tm, d // 2), lambda i: (i + tiles, 0))]


def _combine_kernel(x_ref, sel_ref, y0_ref, y1_ref, o_ref):
    o_ref[...] = _combined(x_ref, sel_ref, y0_ref, y1_ref)


def _combine(pending):
    x2d, sel, buf = pending
    n, d = x2d.shape
    tm = ROW_TILE
    return pl.pallas_call(
        _combine_kernel,
        out_shape=jax.ShapeDtypeStruct((n, d), F32),
        grid=(n // tm,),
        in_specs=_pending_specs(tm, n, d),
        out_specs=pl.BlockSpec((tm, d), lambda i: (i, 0)),
        compiler_params=_params("parallel"),
        name="moe_combine",
    )(x2d, sel, buf, buf)


def _moe(x2d, g, w_rg, b_rg, w_re, b_re, w_gate, w_up, w_down, layer):
    n, d = x2d.shape
    pad = LANES - N_EXPERT_GROUPS - N_EXPERTS
    w_r = jnp.concatenate([w_rg, w_re, jnp.zeros((d, pad), F32)], axis=1).astype(BF16)
    b_r = jnp.concatenate([b_rg, b_re, jnp.zeros((pad,), F32)]).reshape(1, LANES)
    sel, hp = _route(x2d, g.reshape(1, d), w_r, b_r)
    tables, slot_src, slot_dst = _dispatch_plan(sel, n)
    buf = _expert_mlp(hp, w_gate, w_up, w_down, layer, tables, slot_src, slot_dst)
    return x2d, sel, buf


def _fourier_in_kernel(x_ref, sel_ref, y0_ref, y1_ref, g_ref, w_ref, cc_ref, sc_ref, x_out_ref, a_ref, b_ref):
    x = _combined(x_ref, sel_ref, y0_ref, y1_ref)
    x_out_ref[...] = x
    h = _rms(x, g_ref[...]).astype(BF16)
    u = jnp.dot(h, w_ref[...], preferred_element_type=F32).astype(BF16)
    gd = cc_ref.shape[0]
    for gi in range(FOURIER_GROUPS):
        ug = u[:, gi * gd:(gi + 1) * gd]
        a_ref[:, gi * gd:(gi + 1) * gd] = jnp.dot(ug, cc_ref[...], preferred_element_type=F32).astype(BF16)
        b_ref[:, gi * gd:(gi + 1) * gd] = jnp.dot(ug, sc_ref[...], preferred_element_type=F32).astype(BF16)


def _fourier_in(pending, g, w_in, cc, sc):
    x2d, sel, buf = pending
    n, d = x2d.shape
    tm = ROW_TILE // 2
    gd = d // FOURIER_GROUPS
    out = jax.ShapeDtypeStruct((n, d), BF16)
    row = pl.BlockSpec((tm, d), lambda i: (i, 0))
    return pl.pallas_call(
        _fourier_in_kernel,
        out_shape=(jax.ShapeDtypeStruct((n, d), F32), out, out),
        grid=(n // tm,),
        in_specs=_pending_specs(tm, n, d)
                 + [pl.BlockSpec((1, d), lambda i: (0, 0)),
                    pl.BlockSpec((d, d), lambda i: (0, 0)),
                    pl.BlockSpec((gd, gd), lambda i: (0, 0)),
                    pl.BlockSpec((gd, gd), lambda i: (0, 0))],
        out_specs=(row, row, row),
        compiler_params=_params("parallel"),
        name="fourier_in",
    )(x2d, sel, buf, buf, g, w_in, cc, sc)


def _seq_dft_kernel(cs_ref, ss_ref, a_ref, b_ref, f_ref):
    f = jnp.dot(cs_ref[...], a_ref[...], preferred_element_type=F32)
    f = f + jnp.dot(ss_ref[...], b_ref[...], preferred_element_type=F32)
    f_ref[...] = f.astype(BF16)


def _seq_dft(cs, ss, a3, b3):
    batch, seq, d = a3.shape
    tm = min(seq, 1024)
    tn = min(d, 1024)
    return pl.pallas_call(
        _seq_dft_kernel,
        out_shape=jax.ShapeDtypeStruct((batch, seq, d), BF16),
        grid=(batch, d // tn, seq // tm),
        in_specs=[pl.BlockSpec((tm, seq), lambda b, j, i: (i, 0)),
                  pl.BlockSpec((tm, seq), lambda b, j, i: (i, 0)),
                  pl.BlockSpec((None, seq, tn), lambda b, j, i: (b, 0, j)),
                  pl.BlockSpec((None, seq, tn), lambda b, j, i: (b, 0, j))],
        out_specs=pl.BlockSpec((None, tm, tn), lambda b, j, i: (b, i, j)),
        compiler_params=_params("parallel", "parallel", "arbitrary"),
        name="seq_dft",
    )(cs, ss, a3, b3)


def _proj_residual_kernel(f_ref, w_ref, x_ref, o_ref):
    o_ref[...] = x_ref[...] + jnp.dot(f_ref[...], w_ref[...], preferred_element_type=F32)


def _proj_residual(f2d, w, x2d):
    n, d = x2d.shape
    k = f2d.shape[1]
    tm = ROW_TILE
    return pl.pallas_call(
        _proj_residual_kernel,
        out_shape=jax.ShapeDtypeStruct((n, d), F32),
        grid=(n // tm,),
        in_specs=[pl.BlockSpec((tm, k), lambda i: (i, 0)),
                  pl.BlockSpec((k, d), lambda i: (0, 0)),
                  pl.BlockSpec((tm, d), lambda i: (i, 0))],
        out_specs=pl.BlockSpec((tm, d), lambda i: (i, 0)),
        compiler_params=_params("parallel"),
        name="proj_residual",
    )(f2d, w, x2d)


def _rope_tables(seq):
    half = ROT_DIM // 2
    inv_freq = ROPE_THETA ** (-np.arange(0, ROT_DIM, 2, dtype=np.float64) / ROT_DIM)
    ang = np.arange(seq, dtype=np.float64)[:, None] * inv_freq[None, :]
    cos_t = np.ones((seq, LANES))
    sa = np.zeros((seq, LANES))
    sb = np.zeros((seq, LANES))
    cos_t[:, :half] = np.cos(ang)
    cos_t[:, half:ROT_DIM] = np.cos(ang)
    sa[:, half:ROT_DIM] = np.sin(ang)
    sb[:, :half] = -np.sin(ang)
    return [jnp.asarray(t, F32) for t in (cos_t, sa, sb)]


def _dft_tables(n):
    jk = (np.arange(n, dtype=np.int64)[:, None] * np.arange(n, dtype=np.int64)[None, :]) % n
    ang = 2.0 * np.pi * jk.astype(np.float64) / n
    scale = n ** -0.5
    return np.cos(ang) * scale, np.sin(ang) * scale


def _attention_mixer(x2d, batch, seq, norm_g, w_qkv, q_g, k_g, w_out):
    n, d = x2d.shape
    width = N_ATTN_GROUPS * GROUP_WIDTH
    gw = GROUP_WIDTH
    w3 = jnp.stack([jnp.concatenate([w_qkv[:, t * width + gi * gw:t * width + (gi + 1) * gw]
                                     for t in range(3)], axis=1)
                    for gi in range(N_ATTN_GROUPS)]).astype(BF16)
    qg = (q_g * HEAD_DIM ** -0.5).reshape(1, HEAD_DIM)
    kg = k_g.reshape(1, HEAD_DIM)
    cos_t, sa_t, sb_t = _rope_tables(seq)
    qkv = _qkv_project(x2d, norm_g.reshape(1, d), w3, qg, kg, cos_t, sa_t, sb_t, batch, seq)
    outs = [_attention_group(qkv[gi], gi, batch, seq) for gi in range(N_ATTN_GROUPS)]
    return _attn_out_project([o for o, _ in outs], [l for _, l in outs], x2d, w_out.astype(BF16), seq)


def _fourier_mixer(pending, batch, seq, norm_g, w_in, w_out):
    n, d = pending[0].shape
    gd = d // FOURIER_GROUPS
    cc, sc = _dft_tables(gd)
    cs, ss = _dft_tables(seq)
    cc, sc = jnp.asarray(cc, F32).astype(BF16), jnp.asarray(sc, F32).astype(BF16)
    cs, nss = jnp.asarray(cs, F32).astype(BF16), jnp.asarray(-ss, F32).astype(BF16)
    x2d, a, b = _fourier_in(pending, norm_g.reshape(1, d), w_in.astype(BF16), cc, sc)
    f = _seq_dft(cs, nss, a.reshape(batch, seq, d), b.reshape(batch, seq, d))
    return _proj_residual(f.reshape(n, d), w_out.astype(BF16), x2d)


def kernel(x, attn_norm_g, w_qkv, q_norm_g, k_norm_g, w_attn_out, fourier_norm_g, w_fourier_in, w_fourier_out, moe_norm_g, w_router_group, b_router_group, w_router_expert, b_router_expert, w_expert_gate, w_expert_up, w_expert_down):
    batch, seq, d = x.shape
    depth = moe_norm_g.shape[0]
    h = x.reshape(batch * seq, d)
    pending = None
    for i in range(depth):
        j = i // 2
        if i % 2 == 0:
            if pending is not None:
                h = _combine(pending)
            h = _attention_mixer(h, batch, seq, attn_norm_g[j], w_qkv[j], q_norm_g[j], k_norm_g[j], w_attn_out[j])
        else:
            h = _fourier_mixer(pending, batch, seq, fourier_norm_g[j], w_fourier_in[j], w_fourier_out[j])
        pending = _moe(h, moe_norm_g[i], w_router_group[i], b_router_group[i], w_router_expert[i],
                       b_router_expert[i], w_expert_gate, w_expert_up, w_expert_down, i)
    return _combine(pending).reshape(batch, seq, d)
```

```python
import functools

import numpy as np
import jax
import jax.numpy as jnp
from jax import lax
from jax.experimental import pallas as pl
from jax.experimental.pallas import tpu as pltpu

F32 = jnp.float32
BF16 = jnp.bfloat16

HEAD_DIM = 128
HEADS_PER_GROUP = 4
DILATED_PATTERNS = ((128, 1), (512, 4), (2048, 16))
N_ATTN_GROUPS = len(DILATED_PATTERNS)
GROUP_WIDTH = HEADS_PER_GROUP * HEAD_DIM
ROT_DIM = HEAD_DIM // 4
ROPE_THETA = 500000.0
NEG_INF = -1e30
FOURIER_GROUPS = 8
N_EXPERT_GROUPS = 8
EXPERTS_PER_GROUP = 8
N_EXPERTS = N_EXPERT_GROUPS * EXPERTS_PER_GROUP
TOP_K = 2
MOE_BLOCK = 128
GATHER_SLOTS = 3
EPS = 1e-6

LANES = 128
ATTN_Q_BLOCK = 128
ATTN_TILES_PER_ROUND = 2
ROW_TILE = 512
VMEM_LIMIT = 48 * 1024 * 1024


def _params(*sem):
    return pltpu.CompilerParams(dimension_semantics=sem, vmem_limit_bytes=VMEM_LIMIT)


def _rms(x, g):
    ms = jnp.mean(x * x, axis=-1, keepdims=True)
    return x * lax.rsqrt(ms + EPS) * g


def _qkv_kernel(x_ref, g_ref, w_ref, qg_ref, kg_ref, cos_ref, sa_ref, sb_ref, *rest, n_steps):
    o_refs, (h_ref, r_new, r_old, c_ref) = rest[:N_ATTN_GROUPS], rest[N_ATTN_GROUPS:]
    s = pl.program_id(0)
    chunks, tm, _ = c_ref.shape

    @pl.when((s < n_steps) & (lax.rem(s, N_ATTN_GROUPS) == 0))
    def _():
        h_ref[...] = _rms(x_ref[...], g_ref[...]).astype(BF16)

    def project():
        r_new[...] = jnp.dot(h_ref[...], w_ref[...], preferred_element_type=F32)

    def finish(gi):
        cos, sa, sb = cos_ref[...], sa_ref[...], sb_ref[...]
        for j in range(2 * HEADS_PER_GROUP):
            t = r_old[:, j * HEAD_DIM:(j + 1) * HEAD_DIM]
            t = _rms(t, qg_ref[...] if j < HEADS_PER_GROUP else kg_ref[...])
            t = t * cos + pltpu.roll(t, ROT_DIM // 2, 1) * sa + pltpu.roll(t, HEAD_DIM - ROT_DIM // 2, 1) * sb
            c_ref[j] = t
        for j in range(2 * HEADS_PER_GROUP, 3 * HEADS_PER_GROUP):
            c_ref[j] = r_old[:, j * HEAD_DIM:(j + 1) * HEAD_DIM]
        dilation = DILATED_PATTERNS[gi][1]
        for phase in range(dilation):
            for c in range(chunks):
                rows = c_ref[c, pl.ds(phase, tm // dilation, stride=dilation), :]
                col = (phase * chunks + c) * LANES
                o_refs[gi][:, col:col + LANES] = rows.astype(BF16)

    def hand_over():
        r_old[...] = r_new[...]

    @pl.when(s == 0)
    def _():
        project()
        hand_over()

    for gi in range(N_ATTN_GROUPS):
        prev_is_gi = lax.rem(s + N_ATTN_GROUPS - 1, N_ATTN_GROUPS) == gi

        @pl.when((s >= 1) & (s < n_steps) & prev_is_gi)
        def _(gi=gi):
            project()
            finish(gi)
            hand_over()

    pl.when(s == n_steps)(functools.partial(finish, (n_steps - 1) % N_ATTN_GROUPS))


def _qkv_project(x2d, g, w3, qg, kg, cos_t, sa_t, sb_t, batch, seq):
    n, d = x2d.shape
    tm = ROW_TILE
    seq_tiles = seq // tm
    gw3 = 3 * GROUP_WIDTH
    ng = N_ATTN_GROUPS
    n_steps = (n // tm) * ng
    cur = lambda s: jnp.minimum(s, n_steps - 1)
    prev_tile = lambda s: jnp.maximum(s - 1, 0) // ng
    tab = pl.BlockSpec((tm, LANES), lambda s: (prev_tile(s) % seq_tiles, 0))
    dils = [dil for _, dil in DILATED_PATTERNS]

    def out_spec(gi, dil):
        def index(s):
            t = jnp.maximum(s - 1 - gi, 0) // ng
            return (t // seq_tiles, t % seq_tiles, 0)
        return pl.BlockSpec((None, tm // dil, dil * gw3), index)

    return pl.pallas_call(
        functools.partial(_qkv_kernel, n_steps=n_steps),
        out_shape=[jax.ShapeDtypeStruct((batch, seq // dil, dil * gw3), BF16) for dil in dils],
        grid=(n_steps + 1,),
        in_specs=[
            pl.BlockSpec((tm, d), lambda s: (cur(s) // ng, 0)),
            pl.BlockSpec((1, d), lambda s: (0, 0)),
            pl.BlockSpec((None, d, gw3), lambda s: (cur(s) % ng, 0, 0)),
            pl.BlockSpec((1, LANES), lambda s: (0, 0)),
            pl.BlockSpec((1, LANES), lambda s: (0, 0)),
            tab, tab, tab,
        ],
        out_specs=[out_spec(gi, dil) for gi, dil in enumerate(dils)],
        scratch_shapes=[pltpu.VMEM((tm, d), BF16), pltpu.VMEM((tm, gw3), F32), pltpu.VMEM((tm, gw3), F32),
                        pltpu.VMEM((gw3 // LANES, tm, LANES), F32)],
        compiler_params=_params("arbitrary"),
        name="qkv_project",
    )(x2d, g, w3, qg, kg, cos_t, sa_t, sb_t)


def _attn_kernel(qkv_ref, o_ref, lse_ref, *, dilation, length, radius):
    gw = GROUP_WIDTH
    bq = ATTN_Q_BLOCK
    win = min(length, bq + 2 * radius)
    lane = lax.broadcasted_iota(jnp.int32, (bq, LANES), 1)
    tiles = [(r, qb) for r in range(dilation) for qb in range(length // bq)]
    for t0 in range(0, len(tiles), ATTN_TILES_PER_ROUND):
        work = []
        for r, qb in tiles[t0:t0 + ATTN_TILES_PER_ROUND]:
            q0 = qb * bq
            k0 = min(max(q0 - radius, 0), length - win)
            jq = q0 + lax.broadcasted_iota(jnp.int32, (bq, win), 0)
            jk = k0 + lax.broadcasted_iota(jnp.int32, (bq, win), 1)
            valid = jnp.abs(jk - jq) <= radius
            for hh in range(HEADS_PER_GROUP):
                c = r * 3 * gw + hh * HEAD_DIM
                q = qkv_ref[q0:q0 + bq, c:c + HEAD_DIM]
                k = qkv_ref[k0:k0 + win, c + gw:c + gw + HEAD_DIM]
                s = lax.dot_general(q, k, (((1,), (1,)), ((), ())), preferred_element_type=F32)
                work.append((r, q0, k0, hh, jnp.where(valid, s, NEG_INF)))
        soft = []
        for r, q0, k0, hh, s in work:
            m = jnp.max(s, axis=-1, keepdims=True)
            p = jnp.exp(s - m)
            l = jnp.sum(p, axis=-1, keepdims=True)
            soft.append((p.astype(BF16), l, m + jnp.log(l)))
        lse_tile = None
        for (r, q0, k0, hh, _), (p, l, lse) in zip(work, soft):
            c = r * 3 * gw + hh * HEAD_DIM
            v = qkv_ref[k0:k0 + win, c + 2 * gw:c + 2 * gw + HEAD_DIM]
            o = jnp.dot(p, v, preferred_element_type=F32) / l
            oc = r * gw + hh * HEAD_DIM
            o_ref[q0:q0 + bq, oc:oc + HEAD_DIM] = o.astype(BF16)
            lse_tile = jnp.where(lane == hh, lse, jnp.zeros((bq, LANES), F32) if hh == 0 else lse_tile)
            if hh == HEADS_PER_GROUP - 1:
                lse_ref[q0:q0 + bq, r * LANES:(r + 1) * LANES] = lse_tile


def _attention_group(qkv_g, gi, batch, seq):
    window, dilation = DILATED_PATTERNS[gi]
    radius = (window // 2) // dilation
    length = seq // dilation
    gw3 = 3 * GROUP_WIDTH
    kern = functools.partial(_attn_kernel, dilation=dilation, length=length, radius=radius)
    return pl.pallas_call(
        kern,
        out_shape=(jax.ShapeDtypeStruct((batch, length, dilation * GROUP_WIDTH), BF16),
                   jax.ShapeDtypeStruct((batch, length, dilation * LANES), F32)),
        grid=(batch,),
        in_specs=[pl.BlockSpec((None, length, dilation * gw3), lambda b: (b, 0, 0))],
        out_specs=(pl.BlockSpec((None, length, dilation * GROUP_WIDTH), lambda b: (b, 0, 0)),
                   pl.BlockSpec((None, length, dilation * LANES), lambda b: (b, 0, 0))),
        compiler_params=_params("parallel"),
        name=f"band_attention_d{dilation}",
    )(qkv_g)


def _attn_out_kernel(o0, o1, o2, l0, l1, l2, x_ref, w_ref, out_ref, mix_ref, o_rows, l_rows):
    tm = out_ref.shape[0]
    for gi, (o_ref, l_ref) in enumerate(((o0, l0), (o1, l1), (o2, l2))):
        dilation = DILATED_PATTERNS[gi][1]
        for phase in range(dilation):
            dst = pl.ds(phase, tm // dilation, stride=dilation)
            for hh in range(HEADS_PER_GROUP):
                col = phase * GROUP_WIDTH + hh * HEAD_DIM
                o_rows[gi * HEADS_PER_GROUP + hh, dst, :] = o_ref[:, col:col + HEAD_DIM].astype(F32)
            l_rows[gi, dst, :] = l_ref[:, phase * LANES:(phase + 1) * LANES]
    ls = [l_rows[gi] for gi in range(N_ATTN_GROUPS)]
    m = jnp.maximum(jnp.maximum(ls[0], ls[1]), ls[2])
    es = [jnp.exp(l - m) for l in ls]
    den = es[0] + es[1] + es[2]
    for gi in range(N_ATTN_GROUPS):
        alpha = es[gi] / den
        for hh in range(HEADS_PER_GROUP):
            c = hh * HEAD_DIM
            a = alpha[:, hh:hh + 1]
            mix_ref[:, gi * GROUP_WIDTH + c:gi * GROUP_WIDTH + c + HEAD_DIM] = (
                o_rows[gi * HEADS_PER_GROUP + hh] * a).astype(BF16)
    out_ref[...] = x_ref[...] + jnp.dot(mix_ref[...], w_ref[...], preferred_element_type=F32)


def _attn_out_project(os_, lses, x2d, w_out, seq):
    n, d = x2d.shape
    tm = ROW_TILE
    seq_tiles = seq // tm
    width = N_ATTN_GROUPS * GROUP_WIDTH
    dils = [dil for _, dil in DILATED_PATTERNS]
    pm = lambda i: (i // seq_tiles, i % seq_tiles, 0)
    return pl.pallas_call(
        _attn_out_kernel,
        out_shape=jax.ShapeDtypeStruct((n, d), F32),
        grid=(n // tm,),
        in_specs=[pl.BlockSpec((None, tm // dil, dil * GROUP_WIDTH), pm) for dil in dils]
                 + [pl.BlockSpec((None, tm // dil, dil * LANES), pm) for dil in dils]
                 + [pl.BlockSpec((tm, d), lambda i: (i, 0)),
                    pl.BlockSpec((width, d), lambda i: (0, 0))],
        out_specs=pl.BlockSpec((tm, d), lambda i: (i, 0)),
        scratch_shapes=[pltpu.VMEM((tm, width), BF16),
                        pltpu.VMEM((N_ATTN_GROUPS * HEADS_PER_GROUP, tm, HEAD_DIM), F32),
                        pltpu.VMEM((N_ATTN_GROUPS, tm, LANES), F32)],
        compiler_params=_params("parallel"),
        name="attn_out_project",
    )(*os_, *lses, x2d, w_out)


U32 = jnp.uint32
HIGH_HALF = np.uint32(0xFFFF0000)


def _pack_halves(x):
    c = x.shape[1] // 2
    lo = lax.bitcast_convert_type(x[:, :c].astype(BF16).astype(F32), U32)
    hi = lax.bitcast_convert_type(x[:, c:].astype(BF16).astype(F32), U32)
    return (lo >> 16) | (hi & HIGH_HALF)


def _unpack_halves(u):
    lo = lax.bitcast_convert_type(u << 16, F32)
    hi = lax.bitcast_convert_type(u & HIGH_HALF, F32)
    return jnp.concatenate([lo, hi], axis=1)


def _router_kernel(x_ref, g_ref, w_ref, b_ref, sel_ref, hp_ref):
    hf = _rms(x_ref[...], g_ref[...])
    hp_ref[...] = _pack_halves(hf)
    h = hf.astype(BF16)
    logits = jnp.dot(h, w_ref[...], preferred_element_type=F32) + b_ref[...]
    lane = lax.broadcasted_iota(jnp.int32, logits.shape, 1)
    lanef = lane.astype(F32)
    big = float(LANES)
    is_grp = lane < N_EXPERT_GROUPS
    coarse = jnp.where(is_grp, logits, -jnp.inf)
    cmax = jnp.max(coarse, axis=-1, keepdims=True)
    g_sel = jnp.min(jnp.where(coarse == cmax, lanef, big), axis=-1, keepdims=True)
    den = jnp.sum(jnp.where(is_grp, jnp.exp(logits - cmax), 0.0), axis=-1, keepdims=True)
    g_gate = 1.0 / den
    lo = N_EXPERT_GROUPS + g_sel * EXPERTS_PER_GROUP
    in_grp = (lanef >= lo) & (lanef < lo + EXPERTS_PER_GROUP)
    fine = jnp.where(in_grp, logits, -jnp.inf)
    v1 = jnp.max(fine, axis=-1, keepdims=True)
    i1 = jnp.min(jnp.where(fine == v1, lanef, big), axis=-1, keepdims=True)
    fine2 = jnp.where(lanef == i1, -jnp.inf, fine)
    v2 = jnp.max(fine2, axis=-1, keepdims=True)
    i2 = jnp.min(jnp.where(fine2 == v2, lanef, big), axis=-1, keepdims=True)
    e2 = jnp.exp(v2 - v1)
    w1 = g_gate * (1.0 / (1.0 + e2))
    w2 = g_gate * (e2 / (1.0 + e2))
    sel = jnp.where(lane == 0, w1, 0.0)
    sel = jnp.where(lane == 1, w2, sel)
    sel = jnp.where(lane == 2, i1 - N_EXPERT_GROUPS, sel)
    sel = jnp.where(lane == 3, i2 - N_EXPERT_GROUPS, sel)
    sel_ref[...] = sel


def _route(x2d, g, w_r, b_r):
    n, d = x2d.shape
    tm = ROW_TILE
    return pl.pallas_call(
        _router_kernel,
        out_shape=(jax.ShapeDtypeStruct((n, LANES), F32), jax.ShapeDtypeStruct((n, d // 2), U32)),
        grid=(n // tm,),
        in_specs=[pl.BlockSpec((tm, d), lambda i: (i, 0)),
                  pl.BlockSpec((1, d), lambda i: (0, 0)),
                  pl.BlockSpec((d, LANES), lambda i: (0, 0)),
                  pl.BlockSpec((1, LANES), lambda i: (0, 0))],
        out_specs=(pl.BlockSpec((tm, LANES), lambda i: (i, 0)), pl.BlockSpec((tm, d // 2), lambda i: (i, 0))),
        compiler_params=_params("parallel"),
        name="moe_router",
    )(x2d, g, w_r, b_r)


def _expert_kernel(be_ref, first_ref, ws_ref, ne_ref, nu_ref,
                   src_ref, nxt_ref, nx2_ref, dst_ref,
                   h_hbm, wg_hbm, wu_hbm, wd_hbm,
                   buf_hbm,
                   xbuf, ybuf, wgb, wub, wdb, gsem, ssem, wsem, *, layer):
    i = pl.program_id(0)
    n_used = nu_ref[0]
    slot = lax.rem(i, 2)
    gslot = lax.rem(i, GATHER_SLOTS)
    rows = MOE_BLOCK

    def start_gather(idx_ref, s):
        for r in range(rows):
            pltpu.make_async_copy(h_hbm.at[pl.ds(idx_ref[0, 0, r], 1)], xbuf.at[s, pl.ds(r, 1)],
                                  gsem.at[s]).start()

    def wait_gather(s):
        pltpu.make_async_copy(h_hbm.at[pl.ds(0, rows)], xbuf.at[s], gsem.at[s]).wait()

    def start_scatter(s):
        for r in range(rows):
            pltpu.make_async_copy(ybuf.at[s, pl.ds(r, 1)], buf_hbm.at[pl.ds(dst_ref[0, 0, r], 1)],
                                  ssem.at[s]).start(priority=1)

    def wait_scatter(s):
        pltpu.make_async_copy(ybuf.at[s], buf_hbm.at[pl.ds(0, rows)], ssem.at[s]).wait()

    def weight_copies(e, s):
        copies = []
        for hbm, vmem in ((wg_hbm, wgb), (wu_hbm, wub), (wd_hbm, wdb)):
            half = hbm.shape[2] // 2
            for q in range(2):
                part = pl.ds(q * half, half)
                copies.append((pltpu.make_async_copy(hbm.at[layer, e, part], vmem.at[s, part], wsem.at[s]), q))
        return copies

    @pl.when(i == 0)
    def _():
        ybuf[...] = jnp.zeros_like(ybuf)
        start_gather(src_ref, 0)
        start_gather(nxt_ref, 1)
        for c, q in weight_copies(be_ref[0], 0):
            c.start(priority=q)

    @pl.when(i < n_used)
    def _():
        wait_gather(gslot)

        @pl.when(i >= 1)
        def _():
            wait_scatter(slot)

        @pl.when(first_ref[i] == 1)
        def _():
            s = ws_ref[i]
            for c, _ in weight_copies(be_ref[i], s):
                c.wait()

            @pl.when(ne_ref[i] >= 0)
            def _():
                for c, q in weight_copies(ne_ref[i], 1 - s):
                    c.start(priority=q)

        start_scatter(1 - slot)
        ws = ws_ref[i]
        h = _unpack_halves(xbuf[gslot]).astype(BF16)
        gate = jnp.dot(h, wgb[ws].astype(BF16), preferred_element_type=F32)
        up = jnp.dot(h, wub[ws].astype(BF16), preferred_element_type=F32)
        act = (gate * (1.0 / (1.0 + jnp.exp(-gate))) * up).astype(BF16)
        ybuf[slot] = _pack_halves(jnp.dot(act, wdb[ws].astype(BF16), preferred_element_type=F32))
        start_gather(nx2_ref, lax.rem(i + 2, GATHER_SLOTS))

    @pl.when(i == n_used)
    def _():
        wait_scatter(slot)
        start_scatter(1 - slot)
        wait_scatter(1 - slot)
        wait_gather(gslot)
        wait_gather(lax.rem(i + 1, GATHER_SLOTS))


def _expert_mlp(hp, w_gate, w_up, w_down, layer, tables, slot_src, slot_dst):
    n, dp = hp.shape
    d_exp = w_gate.shape[-1]
    d = w_gate.shape[-2]
    rows = MOE_BLOCK
    n_blocks = slot_src.shape[0]
    src3 = slot_src.reshape(n_blocks, 1, rows)
    dst3 = slot_dst.reshape(n_blocks + 1, 1, rows)
    smem = functools.partial(pl.BlockSpec, (1, 1, rows), memory_space=pltpu.SMEM)
    grid_spec = pltpu.PrefetchScalarGridSpec(
        num_scalar_prefetch=len(tables),
        grid=(n_blocks + 1,),
        in_specs=[
            smem(lambda i, *_: (jnp.minimum(i, n_blocks - 1), 0, 0)),
            smem(lambda i, *_: (jnp.minimum(i + 1, n_blocks - 1), 0, 0)),
            smem(lambda i, *_: (jnp.minimum(i + 2, n_blocks - 1), 0, 0)),
            smem(lambda i, *_: (i, 0, 0)),
            pl.BlockSpec(memory_space=pl.ANY),
            pl.BlockSpec(memory_space=pl.ANY),
            pl.BlockSpec(memory_space=pl.ANY),
            pl.BlockSpec(memory_space=pl.ANY),
        ],
        out_specs=pl.BlockSpec(memory_space=pl.ANY),
        scratch_shapes=[pltpu.VMEM((GATHER_SLOTS, rows, dp), U32), pltpu.VMEM((2, rows, dp), U32),
                        pltpu.VMEM((2, d, d_exp), F32), pltpu.VMEM((2, d, d_exp), F32),
                        pltpu.VMEM((2, d_exp, d), F32),
                        pltpu.SemaphoreType.DMA((GATHER_SLOTS,)), pltpu.SemaphoreType.DMA((2,)),
                        pltpu.SemaphoreType.DMA((2,))],
    )
    return pl.pallas_call(
        functools.partial(_expert_kernel, layer=layer),
        out_shape=jax.ShapeDtypeStruct((TOP_K * n + rows, dp), U32),
        grid_spec=grid_spec,
        compiler_params=_params("arbitrary"),
        name="moe_expert_mlp",
    )(*tables, src3, src3, src3, dst3, hp, w_gate, w_up, w_down)


def _dispatch_plan(sel, n):
    n_assign = n * TOP_K
    n_blocks = -(-n_assign // MOE_BLOCK) + N_EXPERTS
    e_flat = sel[:, TOP_K:2 * TOP_K].astype(jnp.int32).reshape(-1)
    experts = jnp.arange(N_EXPERTS, dtype=jnp.int32)
    counts = jnp.sum((e_flat[:, None] == experts[None, :]).astype(jnp.int32), axis=0)
    padded = ((counts + MOE_BLOCK - 1) // MOE_BLOCK) * MOE_BLOCK
    pend = jnp.cumsum(padded)
    pstart = pend - padded
    start = jnp.cumsum(counts) - counts
    order = jnp.argsort(e_flat, stable=True).astype(jnp.int32)
    n_used = pend[-1] // MOE_BLOCK
    blk = jnp.arange(n_blocks, dtype=jnp.int32)
    used = blk < n_used

    def per_block(onehot, table):
        return jnp.sum(jnp.where(onehot, table[None, :], 0), axis=1)

    block_e = jnp.minimum(jnp.sum((pend[None, :] <= (blk * MOE_BLOCK)[:, None]).astype(jnp.int32), axis=1),
                          N_EXPERTS - 1)
    last_e = jnp.sum(jnp.where(blk == n_used - 1, block_e, 0))
    block_e = jnp.where(used, block_e, last_e)
    onehot = block_e[:, None] == experts[None, :]
    first_row = blk * MOE_BLOCK - per_block(onehot, pstart)
    n_valid = jnp.where(used, jnp.clip(per_block(onehot, counts) - first_row, 0, MOE_BLOCK), 0)
    nonempty = counts > 0
    w_slot = lax.rem(jnp.cumsum(nonempty.astype(jnp.int32)) - 1, 2)
    later = nonempty[None, :] & (experts[None, :] > experts[:, None])
    next_e = jnp.min(jnp.where(later, experts[None, :], N_EXPERTS), axis=1)
    next_e = jnp.where(next_e == N_EXPERTS, -1, next_e)
    i32 = lambda v: v.astype(jnp.int32)
    pad1 = lambda v, fill: i32(jnp.concatenate([v, jnp.full((1,), fill, v.dtype)]))
    tables = (pad1(block_e, 0), pad1(i32(used & (first_row == 0)), 0),
              pad1(per_block(onehot, w_slot), 0), pad1(per_block(onehot, next_e), -1),
              i32(n_used.reshape(1)))
    in_blk = jnp.arange(MOE_BLOCK, dtype=jnp.int32)[None, :]
    sorted_pos = (per_block(onehot, start) + first_row)[:, None] + in_blk
    a = order[jnp.clip(sorted_pos, 0, n_assign - 1)]
    valid = in_blk < n_valid[:, None]
    tok = a // TOP_K
    tail = jnp.broadcast_to(TOP_K * n + in_blk, a.shape)
    slot_src = i32(jnp.where(valid, tok, 0))
    slot_dst = jnp.where(valid, (a % TOP_K) * n + tok, tail)
    slot_dst = i32(jnp.concatenate([tail[:1], slot_dst]))
    return tables, slot_src, slot_dst


def _combined(x_ref, sel_ref, y0_ref, y1_ref):
    sel = sel_ref[...]
    y0, y1 = _unpack_halves(y0_ref[...]), _unpack_halves(y1_ref[...])
    return x_ref[...] + (y0 * sel[:, 0:1] + y1 * sel[:, 1:2])


def _pending_specs(tm, n, d):
    tiles = n // tm
    return [pl.BlockSpec((tm, d), lambda i: (i, 0)),
            pl.BlockSpec((tm, LANES), lambda i: (i, 0)),
            pl.BlockSpec((tm, d // 2), lambda i: (i, 0)),
            pl.BlockSpec((tm, d // 2), lambda i: (i + tiles, 0))]


def _combine_kernel(x_ref, sel_ref, y0_ref, y1_ref, o_ref):
    o_ref[...] = _combined(x_ref, sel_ref, y0_ref, y1_ref)


def _combine(pending):
    x2d, sel, buf = pending
    n, d = x2d.shape
    tm = ROW_TILE
    return pl.pallas_call(
        _combine_kernel,
        out_shape=jax.ShapeDtypeStruct((n, d), F32),
        grid=(n // tm,),
        in_specs=_pending_specs(tm, n, d),
        out_specs=pl.BlockSpec((tm, d), lambda i: (i, 0)),
        compiler_params=_params("parallel"),
        name="moe_combine",
    )(x2d, sel, buf, buf)


def _moe(x2d, g, w_rg, b_rg, w_re, b_re, w_gate, w_up, w_down, layer):
    n, d = x2d.shape
    pad = LANES - N_EXPERT_GROUPS - N_EXPERTS
    w_r = jnp.concatenate([w_rg, w_re, jnp.zeros((d, pad), F32)], axis=1).astype(BF16)
    b_r = jnp.concatenate([b_rg, b_re, jnp.zeros((pad,), F32)]).reshape(1, LANES)
    sel, hp = _route(x2d, g.reshape(1, d), w_r, b_r)
    tables, slot_src, slot_dst = _dispatch_plan(sel, n)
    buf = _expert_mlp(hp, w_gate, w_up, w_down, layer, tables, slot_src, slot_dst)
    return x2d, sel, buf


def _fourier_in_kernel(x_ref, sel_ref, y0_ref, y1_ref, g_ref, w_ref, cc_ref, sc_ref, x_out_ref, a_ref, b_ref):
    x = _combined(x_ref, sel_ref, y0_ref, y1_ref)
    x_out_ref[...] = x
    h = _rms(x, g_ref[...]).astype(BF16)
    u = jnp.dot(h, w_ref[...], preferred_element_type=F32).astype(BF16)
    gd = cc_ref.shape[0]
    for gi in range(FOURIER_GROUPS):
        ug = u[:, gi * gd:(gi + 1) * gd]
        a_ref[:, gi * gd:(gi + 1) * gd] = jnp.dot(ug, cc_ref[...], preferred_element_type=F32).astype(BF16)
        b_ref[:, gi * gd:(gi + 1) * gd] = jnp.dot(ug, sc_ref[...], preferred_element_type=F32).astype(BF16)


def _fourier_in(pending, g, w_in, cc, sc):
    x2d, sel, buf = pending
    n, d = x2d.shape
    tm = ROW_TILE // 2
    gd = d // FOURIER_GROUPS
    out = jax.ShapeDtypeStruct((n, d), BF16)
    row = pl.BlockSpec((tm, d), lambda i: (i, 0))
    return pl.pallas_call(
        _fourier_in_kernel,
        out_shape=(jax.ShapeDtypeStruct((n, d), F32), out, out),
        grid=(n // tm,),
        in_specs=_pending_specs(tm, n, d)
                 + [pl.BlockSpec((1, d), lambda i: (0, 0)),
                    pl.BlockSpec((d, d), lambda i: (0, 0)),
                    pl.BlockSpec((gd, gd), lambda i: (0, 0)),
                    pl.BlockSpec((gd, gd), lambda i: (0, 0))],
        out_specs=(row, row, row),
        compiler_params=_params("parallel"),
        name="fourier_in",
    )(x2d, sel, buf, buf, g, w_in, cc, sc)


def _seq_dft_kernel(cs_ref, ss_ref, a_ref, b_ref, f_ref):
    f = jnp.dot(cs_ref[...], a_ref[...], preferred_element_type=F32)
    f = f + jnp.dot(ss_ref[...], b_ref[...], preferred_element_type=F32)
    f_ref[...] = f.astype(BF16)


def _seq_dft(cs, ss, a3, b3):
    batch, seq, d = a3.shape
    tm = min(seq, 1024)
    tn = min(d, 1024)
    return pl.pallas_call(
        _seq_dft_kernel,
        out_shape=jax.ShapeDtypeStruct((batch, seq, d), BF16),
        grid=(batch, d // tn, seq // tm),
        in_specs=[pl.BlockSpec((tm, seq), lambda b, j, i: (i, 0)),
                  pl.BlockSpec((tm, seq), lambda b, j, i: (i, 0)),
                  pl.BlockSpec((None, seq, tn), lambda b, j, i: (b, 0, j)),
                  pl.BlockSpec((None, seq, tn), lambda b, j, i: (b, 0, j))],
        out_specs=pl.BlockSpec((None, tm, tn), lambda b, j, i: (b, i, j)),
        compiler_params=_params("parallel", "parallel", "arbitrary"),
        name="seq_dft",
    )(cs, ss, a3, b3)


def _proj_residual_kernel(f_ref, w_ref, x_ref, o_ref):
    o_ref[...] = x_ref[...] + jnp.dot(f_ref[...], w_ref[...], preferred_element_type=F32)


def _proj_residual(f2d, w, x2d):
    n, d = x2d.shape
    k = f2d.shape[1]
    tm = ROW_TILE
    return pl.pallas_call(
        _proj_residual_kernel,
        out_shape=jax.ShapeDtypeStruct((n, d), F32),
        grid=(n // tm,),
        in_specs=[pl.BlockSpec((tm, k), lambda i: (i, 0)),
                  pl.BlockSpec((k, d), lambda i: (0, 0)),
                  pl.BlockSpec((tm, d), lambda i: (i, 0))],
        out_specs=pl.BlockSpec((tm, d), lambda i: (i, 0)),
        compiler_params=_params("parallel"),
        name="proj_residual",
    )(f2d, w, x2d)


def _rope_tables(seq):
    half = ROT_DIM // 2
    inv_freq = ROPE_THETA ** (-np.arange(0, ROT_DIM, 2, dtype=np.float64) / ROT_DIM)
    ang = np.arange(seq, dtype=np.float64)[:, None] * inv_freq[None, :]
    cos_t = np.ones((seq, LANES))
    sa = np.zeros((seq, LANES))
    sb = np.zeros((seq, LANES))
    cos_t[:, :half] = np.cos(ang)
    cos_t[:, half:ROT_DIM] = np.cos(ang)
    sa[:, half:ROT_DIM] = np.sin(ang)
    sb[:, :half] = -np.sin(ang)
    return [jnp.asarray(t, F32) for t in (cos_t, sa, sb)]


def _dft_tables(n):
    jk = (np.arange(n, dtype=np.int64)[:, None] * np.arange(n, dtype=np.int64)[None, :]) % n
    ang = 2.0 * np.pi * jk.astype(np.float64) / n
    scale = n ** -0.5
    return np.cos(ang) * scale, np.sin(ang) * scale


def _attention_mixer(x2d, batch, seq, norm_g, w_qkv, q_g, k_g, w_out):
    n, d = x2d.shape
    width = N_ATTN_GROUPS * GROUP_WIDTH
    gw = GROUP_WIDTH
    w3 = jnp.stack([jnp.concatenate([w_qkv[:, t * width + gi * gw:t * width + (gi + 1) * gw]
                                     for t in range(3)], axis=1)
                    for gi in range(N_ATTN_GROUPS)]).astype(BF16)
    qg = (q_g * HEAD_DIM ** -0.5).reshape(1, HEAD_DIM)
    kg = k_g.reshape(1, HEAD_DIM)
    cos_t, sa_t, sb_t = _rope_tables(seq)
    qkv = _qkv_project(x2d, norm_g.reshape(1, d), w3, qg, kg, cos_t, sa_t, sb_t, batch, seq)
    outs = [_attention_group(qkv[gi], gi, batch, seq) for gi in range(N_ATTN_GROUPS)]
    return _attn_out_project([o for o, _ in outs], [l for _, l in outs], x2d, w_out.astype(BF16), seq)


def _fourier_mixer(pending, batch, seq, norm_g, w_in, w_out):
    n, d = pending[0].shape
    gd = d // FOURIER_GROUPS
    cc, sc = _dft_tables(gd)
    cs, ss = _dft_tables(seq)
    cc, sc = jnp.asarray(cc, F32).astype(BF16), jnp.asarray(sc, F32).astype(BF16)
    cs, nss = jnp.asarray(cs, F32).astype(BF16), jnp.asarray(-ss, F32).astype(BF16)
    x2d, a, b = _fourier_in(pending, norm_g.reshape(1, d), w_in.astype(BF16), cc, sc)
    f = _seq_dft(cs, nss, a.reshape(batch, seq, d), b.reshape(batch, seq, d))
    return _proj_residual(f.reshape(n, d), w_out.astype(BF16), x2d)


def kernel(x, attn_norm_g, w_qkv, q_norm_g, k_norm_g, w_attn_out, fourier_norm_g, w_fourier_in, w_fourier_out, moe_norm_g, w_router_group, b_router_group, w_router_expert, b_router_expert, w_expert_gate, w_expert_up, w_expert_down):
    batch, seq, d = x.shape
    depth = moe_norm_g.shape[0]
    h = x.reshape(batch * seq, d)
    pending = None
    for i in range(depth):
        j = i // 2
        if i % 2 == 0:
            if pending is not None:
                h = _combine(pending)
            h = _attention_mixer(h, batch, seq, attn_norm_g[j], w_qkv[j], q_norm_g[j], k_norm_g[j], w_attn_out[j])
        else:
            h = _fourier_mixer(pending, batch, seq, fourier_norm_g[j], w_fourier_in[j], w_fourier_out[j])
        pending = _moe(h, moe_norm_g[i], w_router_group[i], b_router_group[i], w_router_expert[i],
                       b_router_expert[i], w_expert_gate, w_expert_up, w_expert_down, i)
    return _combine(pending).reshape(batch, seq, d)
```

```python
import functools

import numpy as np
import jax
import jax.numpy as jnp
from jax import lax
from jax.experimental import pallas as pl
from jax.experimental.pallas import tpu as pltpu

F32 = jnp.float32
BF16 = jnp.bfloat16

HEAD_DIM = 128
HEADS_PER_GROUP = 4
DILATED_PATTERNS = ((128, 1), (512, 4), (2048, 16))
N_ATTN_GROUPS = len(DILATED_PATTERNS)
GROUP_WIDTH = HEADS_PER_GROUP * HEAD_DIM
ROT_DIM = HEAD_DIM // 4
ROPE_THETA = 500000.0
NEG_INF = -1e30
FOURIER_GROUPS = 8
N_EXPERT_GROUPS = 8
EXPERTS_PER_GROUP = 8
N_EXPERTS = N_EXPERT_GROUPS * EXPERTS_PER_GROUP
TOP_K = 2
MOE_BLOCK = 128
GATHER_SLOTS = 3
EPS = 1e-6

LANES = 128
ATTN_Q_BLOCK = 128
ATTN_TILES_PER_ROUND = 2
ROW_TILE = 512
VMEM_LIMIT = 48 * 1024 * 1024


def _params(*sem):
    return pltpu.CompilerParams(dimension_semantics=sem, vmem_limit_bytes=VMEM_LIMIT)


def _rms(x, g):
    ms = jnp.mean(x * x, axis=-1, keepdims=True)
    return x * lax.rsqrt(ms + EPS) * g


def _qkv_kernel(x_ref, g_ref, w_ref, qg_ref, kg_ref, cos_ref, sa_ref, sb_ref, *rest, n_steps):
    o_refs, (h_ref, r_new, r_old, c_ref) = rest[:N_ATTN_GROUPS], rest[N_ATTN_GROUPS:]
    s = pl.program_id(0)
    chunks, tm, _ = c_ref.shape

    @pl.when((s < n_steps) & (lax.rem(s, N_ATTN_GROUPS) == 0))
    def _():
        h_ref[...] = _rms(x_ref[...], g_ref[...]).astype(BF16)

    def project():
        r_new[...] = jnp.dot(h_ref[...], w_ref[...], preferred_element_type=F32)

    def finish(gi):
        cos, sa, sb = cos_ref[...], sa_ref[...], sb_ref[...]
        for j in range(2 * HEADS_PER_GROUP):
            t = r_old[:, j * HEAD_DIM:(j + 1) * HEAD_DIM]
            t = _rms(t, qg_ref[...] if j < HEADS_PER_GROUP else kg_ref[...])
            t = t * cos + pltpu.roll(t, ROT_DIM // 2, 1) * sa + pltpu.roll(t, HEAD_DIM - ROT_DIM // 2, 1) * sb
            c_ref[j] = t
        for j in range(2 * HEADS_PER_GROUP, 3 * HEADS_PER_GROUP):
            c_ref[j] = r_old[:, j * HEAD_DIM:(j + 1) * HEAD_DIM]
        dilation = DILATED_PATTERNS[gi][1]
        for phase in range(dilation):
            for c in range(chunks):
                rows = c_ref[c, pl.ds(phase, tm // dilation, stride=dilation), :]
                col = (phase * chunks + c) * LANES
                o_refs[gi][:, col:col + LANES] = rows.astype(BF16)

    def hand_over():
        r_old[...] = r_new[...]

    @pl.when(s == 0)
    def _():
        project()
        hand_over()

    for gi in range(N_ATTN_GROUPS):
        prev_is_gi = lax.rem(s + N_ATTN_GROUPS - 1, N_ATTN_GROUPS) == gi

        @pl.when((s >= 1) & (s < n_steps) & prev_is_gi)
        def _(gi=gi):
            project()
            finish(gi)
            hand_over()

    pl.when(s == n_steps)(functools.partial(finish, (n_steps - 1) % N_ATTN_GROUPS))


def _qkv_project(x2d, g, w3, qg, kg, cos_t, sa_t, sb_t, batch, seq):
    n, d = x2d.shape
    tm = ROW_TILE
    seq_tiles = seq // tm
    gw3 = 3 * GROUP_WIDTH
    ng = N_ATTN_GROUPS
    n_steps = (n // tm) * ng
    cur = lambda s: jnp.minimum(s, n_steps - 1)
    prev_tile = lambda s: jnp.maximum(s - 1, 0) // ng
    tab = pl.BlockSpec((tm, LANES), lambda s: (prev_tile(s) % seq_tiles, 0))
    dils = [dil for _, dil in DILATED_PATTERNS]

    def out_spec(gi, dil):
        def index(s):
            t = jnp.maximum(s - 1 - gi, 0) // ng
            return (t // seq_tiles, t % seq_tiles, 0)
        return pl.BlockSpec((None, tm // dil, dil * gw3), index)

    return pl.pallas_call(
        functools.partial(_qkv_kernel, n_steps=n_steps),
        out_shape=[jax.ShapeDtypeStruct((batch, seq // dil, dil * gw3), BF16) for dil in dils],
        grid=(n_steps + 1,),
        in_specs=[
            pl.BlockSpec((tm, d), lambda s: (cur(s) // ng, 0)),
            pl.BlockSpec((1, d), lambda s: (0, 0)),
            pl.BlockSpec((None, d, gw3), lambda s: (cur(s) % ng, 0, 0)),
            pl.BlockSpec((1, LANES), lambda s: (0, 0)),
            pl.BlockSpec((1, LANES), lambda s: (0, 0)),
            tab, tab, tab,
        ],
        out_specs=[out_spec(gi, dil) for gi, dil in enumerate(dils)],
        scratch_shapes=[pltpu.VMEM((tm, d), BF16), pltpu.VMEM((tm, gw3), F32), pltpu.VMEM((tm, gw3), F32),
                        pltpu.VMEM((gw3 // LANES, tm, LANES), F32)],
        compiler_params=_params("arbitrary"),
        name="qkv_project",
    )(x2d, g, w3, qg, kg, cos_t, sa_t, sb_t)


def _attn_kernel(qkv_ref, o_ref, lse_ref, *, dilation, length, radius):
    gw = GROUP_WIDTH
    bq = ATTN_Q_BLOCK
    win = min(length, bq + 2 * radius)
    lane = lax.broadcasted_iota(jnp.int32, (bq, LANES), 1)
    tiles = [(r, qb) for r in range(dilation) for qb in range(length // bq)]
    for t0 in range(0, len(tiles), ATTN_TILES_PER_ROUND):
        work = []
        for r, qb in tiles[t0:t0 + ATTN_TILES_PER_ROUND]:
            q0 = qb * bq
            k0 = min(max(q0 - radius, 0), length - win)
            jq = q0 + lax.broadcasted_iota(jnp.int32, (bq, win), 0)
            jk = k0 + lax.broadcasted_iota(jnp.int32, (bq, win), 1)
            valid = jnp.abs(jk - jq) <= radius
            for hh in range(HEADS_PER_GROUP):
                c = r * 3 * gw + hh * HEAD_DIM
                q = qkv_ref[q0:q0 + bq, c:c + HEAD_DIM]
                k = qkv_ref[k0:k0 + win, c + gw:c + gw + HEAD_DIM]
                s = lax.dot_general(q, k, (((1,), (1,)), ((), ())), preferred_element_type=F32)
                work.append((r, q0, k0, hh, jnp.where(valid, s, NEG_INF)))
        soft = []
        for r, q0, k0, hh, s in work:
            m = jnp.max(s, axis=-1, keepdims=True)
            p = jnp.exp(s - m)
            l = jnp.sum(p, axis=-1, keepdims=True)
            soft.append((p.astype(BF16), l, m + jnp.log(l)))
        lse_tile = None
        for (r, q0, k0, hh, _), (p, l, lse) in zip(work, soft):
            c = r * 3 * gw + hh * HEAD_DIM
            v = qkv_ref[k0:k0 + win, c + 2 * gw:c + 2 * gw + HEAD_DIM]
            o = jnp.dot(p, v, preferred_element_type=F32) / l
            oc = r * gw + hh * HEAD_DIM
            o_ref[q0:q0 + bq, oc:oc + HEAD_DIM] = o.astype(BF16)
            lse_tile = jnp.where(lane == hh, lse, jnp.zeros((bq, LANES), F32) if hh == 0 else lse_tile)
            if hh == HEADS_PER_GROUP - 1:
                lse_ref[q0:q0 + bq, r * LANES:(r + 1) * LANES] = lse_tile


def _attention_group(qkv_g, gi, batch, seq):
    window, dilation = DILATED_PATTERNS[gi]
    radius = (window // 2) // dilation
    length = seq // dilation
    gw3 = 3 * GROUP_WIDTH
    kern = functools.partial(_attn_kernel, dilation=dilation, length=length, radius=radius)
    return pl.pallas_call(
        kern,
        out_shape=(jax.ShapeDtypeStruct((batch, length, dilation * GROUP_WIDTH), BF16),
                   jax.ShapeDtypeStruct((batch, length, dilation * LANES), F32)),
        grid=(batch,),
        in_specs=[pl.BlockSpec((None, length, dilation * gw3), lambda b: (b, 0, 0))],
        out_specs=(pl.BlockSpec((None, length, dilation * GROUP_WIDTH), lambda b: (b, 0, 0)),
                   pl.BlockSpec((None, length, dilation * LANES), lambda b: (b, 0, 0))),
        compiler_params=_params("parallel"),
        name=f"band_attention_d{dilation}",
    )(qkv_g)


def _attn_out_kernel(o0, o1, o2, l0, l1, l2, x_ref, w_ref, out_ref, mix_ref, o_rows, l_rows):
    tm = out_ref.shape[0]
    for gi, (o_ref, l_ref) in enumerate(((o0, l0), (o1, l1), (o2, l2))):
        dilation = DILATED_PATTERNS[gi][1]
        for phase in range(dilation):
            dst = pl.ds(phase, tm // dilation, stride=dilation)
            for hh in range(HEADS_PER_GROUP):
                col = phase * GROUP_WIDTH + hh * HEAD_DIM
                o_rows[gi * HEADS_PER_GROUP + hh, dst, :] = o_ref[:, col:col + HEAD_DIM].astype(F32)
            l_rows[gi, dst, :] = l_ref[:, phase * LANES:(phase + 1) * LANES]
    ls = [l_rows[gi] for gi in range(N_ATTN_GROUPS)]
    m = jnp.maximum(jnp.maximum(ls[0], ls[1]), ls[2])
    es = [jnp.exp(l - m) for l in ls]
    den = es[0] + es[1] + es[2]
    for gi in range(N_ATTN_GROUPS):
        alpha = es[gi] / den
        for hh in range(HEADS_PER_GROUP):
            c = hh * HEAD_DIM
            a = alpha[:, hh:hh + 1]
            mix_ref[:, gi * GROUP_WIDTH + c:gi * GROUP_WIDTH + c + HEAD_DIM] = (
                o_rows[gi * HEADS_PER_GROUP + hh] * a).astype(BF16)
    out_ref[...] = x_ref[...] + jnp.dot(mix_ref[...], w_ref[...], preferred_element_type=F32)


def _attn_out_project(os_, lses, x2d, w_out, seq):
    n, d = x2d.shape
    tm = ROW_TILE
    seq_tiles = seq // tm
    width = N_ATTN_GROUPS * GROUP_WIDTH
    dils = [dil for _, dil in DILATED_PATTERNS]
    pm = lambda i: (i // seq_tiles, i % seq_tiles, 0)
    return pl.pallas_call(
        _attn_out_kernel,
        out_shape=jax.ShapeDtypeStruct((n, d), F32),
        grid=(n // tm,),
        in_specs=[pl.BlockSpec((None, tm // dil, dil * GROUP_WIDTH), pm) for dil in dils]
                 + [pl.BlockSpec((None, tm // dil, dil * LANES), pm) for dil in dils]
                 + [pl.BlockSpec((tm, d), lambda i: (i, 0)),
                    pl.BlockSpec((width, d), lambda i: (0, 0))],
        out_specs=pl.BlockSpec((tm, d), lambda i: (i, 0)),
        scratch_shapes=[pltpu.VMEM((tm, width), BF16),
                        pltpu.VMEM((N_ATTN_GROUPS * HEADS_PER_GROUP, tm, HEAD_DIM), F32),
                        pltpu.VMEM((N_ATTN_GROUPS, tm, LANES), F32)],
        compiler_params=_params("parallel"),
        name="attn_out_project",
    )(*os_, *lses, x2d, w_out)


U32 = jnp.uint32
HIGH_HALF = np.uint32(0xFFFF0000)


def _pack_halves(x):
    c = x.shape[1] // 2
    lo = lax.bitcast_convert_type(x[:, :c].astype(BF16).astype(F32), U32)
    hi = lax.bitcast_convert_type(x[:, c:].astype(BF16).astype(F32), U32)
    return (lo >> 16) | (hi & HIGH_HALF)


def _unpack_halves(u):
    lo = lax.bitcast_convert_type(u << 16, F32)
    hi = lax.bitcast_convert_type(u & HIGH_HALF, F32)
    return jnp.concatenate([lo, hi], axis=1)


def _router_kernel(x_ref, g_ref, w_ref, b_ref, sel_ref, hp_ref):
    hf = _rms(x_ref[...], g_ref[...])
    hp_ref[...] = _pack_halves(hf)
    h = hf.astype(BF16)
    logits = jnp.dot(h, w_ref[...], preferred_element_type=F32) + b_ref[...]
    lane = lax.broadcasted_iota(jnp.int32, logits.shape, 1)
    lanef = lane.astype(F32)
    big = float(LANES)
    is_grp = lane < N_EXPERT_GROUPS
    coarse = jnp.where(is_grp, logits, -jnp.inf)
    cmax = jnp.max(coarse, axis=-1, keepdims=True)
    g_sel = jnp.min(jnp.where(coarse == cmax, lanef, big), axis=-1, keepdims=True)
    den = jnp.sum(jnp.where(is_grp, jnp.exp(logits - cmax), 0.0), axis=-1, keepdims=True)
    g_gate = 1.0 / den
    lo = N_EXPERT_GROUPS + g_sel * EXPERTS_PER_GROUP
    in_grp = (lanef >= lo) & (lanef < lo + EXPERTS_PER_GROUP)
    fine = jnp.where(in_grp, logits, -jnp.inf)
    v1 = jnp.max(fine, axis=-1, keepdims=True)
    i1 = jnp.min(jnp.where(fine == v1, lanef, big), axis=-1, keepdims=True)
    fine2 = jnp.where(lanef == i1, -jnp.inf, fine)
    v2 = jnp.max(fine2, axis=-1, keepdims=True)
    i2 = jnp.min(jnp.where(fine2 == v2, lanef, big), axis=-1, keepdims=True)
    e2 = jnp.exp(v2 - v1)
    w1 = g_gate * (1.0 / (1.0 + e2))
    w2 = g_gate * (e2 / (1.0 + e2))
    sel = jnp.where(lane == 0, w1, 0.0)
    sel = jnp.where(lane == 1, w2, sel)
    sel = jnp.where(lane == 2, i1 - N_EXPERT_GROUPS, sel)
    sel = jnp.where(lane == 3, i2 - N_EXPERT_GROUPS, sel)
    sel_ref[...] = sel


def _route(x2d, g, w_r, b_r):
    n, d = x2d.shape
    tm = ROW_TILE
    return pl.pallas_call(
        _router_kernel,
        out_shape=(jax.ShapeDtypeStruct((n, LANES), F32), jax.ShapeDtypeStruct((n, d // 2), U32)),
        grid=(n // tm,),
        in_specs=[pl.BlockSpec((tm, d), lambda i: (i, 0)),
                  pl.BlockSpec((1, d), lambda i: (0, 0)),
                  pl.BlockSpec((d, LANES), lambda i: (0, 0)),
                  pl.BlockSpec((1, LANES), lambda i: (0, 0))],
        out_specs=(pl.BlockSpec((tm, LANES), lambda i: (i, 0)), pl.BlockSpec((tm, d // 2), lambda i: (i, 0))),
        compiler_params=_params("parallel"),
        name="moe_router",
    )(x2d, g, w_r, b_r)


def _expert_kernel(be_ref, first_ref, ws_ref, ne_ref, nu_ref,
                   src_ref, nxt_ref, nx2_ref, dst_ref,
                   h_hbm, wg_hbm, wu_hbm, wd_hbm,
                   buf_hbm,
                   xbuf, ybuf, wgb, wub, wdb, gsem, ssem, wsem, *, layer):
    i = pl.program_id(0)
    n_used = nu_ref[0]
    slot = lax.rem(i, 2)
    gslot = lax.rem(i, GATHER_SLOTS)
    rows = MOE_BLOCK

    def start_gather(idx_ref, s):
        for r in range(rows):
            pltpu.make_async_copy(h_hbm.at[pl.ds(idx_ref[0, 0, r], 1)], xbuf.at[s, pl.ds(r, 1)],
                                  gsem.at[s]).start()

    def wait_gather(s):
        pltpu.make_async_copy(h_hbm.at[pl.ds(0, rows)], xbuf.at[s], gsem.at[s]).wait()

    def start_scatter(s):
        for r in range(rows):
            pltpu.make_async_copy(ybuf.at[s, pl.ds(r, 1)], buf_hbm.at[pl.ds(dst_ref[0, 0, r], 1)],
                                  ssem.at[s]).start(priority=1)

    def wait_scatter(s):
        pltpu.make_async_copy(ybuf.at[s], buf_hbm.at[pl.ds(0, rows)], ssem.at[s]).wait()

    def weight_copies(e, s):
        copies = []
        for hbm, vmem in ((wg_hbm, wgb), (wu_hbm, wub), (wd_hbm, wdb)):
            half = hbm.shape[2] // 2
            for q in range(2):
                part = pl.ds(q * half, half)
                copies.append((pltpu.make_async_copy(hbm.at[layer, e, part], vmem.at[s, part], wsem.at[s]), q))
        return copies

    @pl.when(i == 0)
    def _():
        ybuf[...] = jnp.zeros_like(ybuf)
        start_gather(src_ref, 0)
        start_gather(nxt_ref, 1)
        for c, q in weight_copies(be_ref[0], 0):
            c.start(priority=q)

    @pl.when(i < n_used)
    def _():
        wait_gather(gslot)

        @pl.when(i >= 1)
        def _():
            wait_scatter(slot)

        @pl.when(first_ref[i] == 1)
        def _():
            s = ws_ref[i]
            for c, _ in weight_copies(be_ref[i], s):
                c.wait()

            @pl.when(ne_ref[i] >= 0)
            def _():
                for c, q in weight_copies(ne_ref[i], 1 - s):
                    c.start(priority=q)

        start_scatter(1 - slot)
        ws = ws_ref[i]
        h = _unpack_halves(xbuf[gslot]).astype(BF16)
        gate = jnp.dot(h, wgb[ws].astype(BF16), preferred_element_type=F32)
        up = jnp.dot(h, wub[ws].astype(BF16), preferred_element_type=F32)
        act = (gate * (1.0 / (1.0 + jnp.exp(-gate))) * up).astype(BF16)
        ybuf[slot] = _pack_halves(jnp.dot(act, wdb[ws].astype(BF16), preferred_element_type=F32))
        start_gather(nx2_ref, lax.rem(i + 2, GATHER_SLOTS))

    @pl.when(i == n_used)
    def _():
        wait_scatter(slot)
        start_scatter(1 - slot)
        wait_scatter(1 - slot)
        wait_gather(gslot)
        wait_gather(lax.rem(i + 1, GATHER_SLOTS))


def _expert_mlp(hp, w_gate, w_up, w_down, layer, tables, slot_src, slot_dst):
    n, dp = hp.shape
    d_exp = w_gate.shape[-1]
    d = w_gate.shape[-2]
    rows = MOE_BLOCK
    n_blocks = slot_src.shape[0]
    src3 = slot_src.reshape(n_blocks, 1, rows)
    dst3 = slot_dst.reshape(n_blocks + 1, 1, rows)
    smem = functools.partial(pl.BlockSpec, (1, 1, rows), memory_space=pltpu.SMEM)
    grid_spec = pltpu.PrefetchScalarGridSpec(
        num_scalar_prefetch=len(tables),
        grid=(n_blocks + 1,),
        in_specs=[
            smem(lambda i, *_: (jnp.minimum(i, n_blocks - 1), 0, 0)),
            smem(lambda i, *_: (jnp.minimum(i + 1, n_blocks - 1), 0, 0)),
            smem(lambda i, *_: (jnp.minimum(i + 2, n_blocks - 1), 0, 0)),
            smem(lambda i, *_: (i, 0, 0)),
            pl.BlockSpec(memory_space=pl.ANY),
            pl.BlockSpec(memory_space=pl.ANY),
            pl.BlockSpec(memory_space=pl.ANY),
            pl.BlockSpec(memory_space=pl.ANY),
        ],
        out_specs=pl.BlockSpec(memory_space=pl.ANY),
        scratch_shapes=[pltpu.VMEM((GATHER_SLOTS, rows, dp), U32), pltpu.VMEM((2, rows, dp), U32),
                        pltpu.VMEM((2, d, d_exp), F32), pltpu.VMEM((2, d, d_exp), F32),
                        pltpu.VMEM((2, d_exp, d), F32),
                        pltpu.SemaphoreType.DMA((GATHER_SLOTS,)), pltpu.SemaphoreType.DMA((2,)),
                        pltpu.SemaphoreType.DMA((2,))],
    )
    return pl.pallas_call(
        functools.partial(_expert_kernel, layer=layer),
        out_shape=jax.ShapeDtypeStruct((TOP_K * n + rows, dp), U32),
        grid_spec=grid_spec,
        compiler_params=_params("arbitrary"),
        name="moe_expert_mlp",
    )(*tables, src3, src3, src3, dst3, hp, w_gate, w_up, w_down)


def _dispatch_plan(sel, n):
    n_assign = n * TOP_K
    n_blocks = -(-n_assign // MOE_BLOCK) + N_EXPERTS
    e_flat = sel[:, TOP_K:2 * TOP_K].astype(jnp.int32).reshape(-1)
    experts = jnp.arange(N_EXPERTS, dtype=jnp.int32)
    counts = jnp.sum((e_flat[:, None] == experts[None, :]).astype(jnp.int32), axis=0)
    padded = ((counts + MOE_BLOCK - 1) // MOE_BLOCK) * MOE_BLOCK
    pend = jnp.cumsum(padded)
    pstart = pend - padded
    start = jnp.cumsum(counts) - counts
    order = jnp.argsort(e_flat, stable=True).astype(jnp.int32)
    n_used = pend[-1] // MOE_BLOCK
    blk = jnp.arange(n_blocks, dtype=jnp.int32)
    used = blk < n_used

    def per_block(onehot, table):
        return jnp.sum(jnp.where(onehot, table[None, :], 0), axis=1)

    block_e = jnp.minimum(jnp.sum((pend[None, :] <= (blk * MOE_BLOCK)[:, None]).astype(jnp.int32), axis=1),
                          N_EXPERTS - 1)
    last_e = jnp.sum(jnp.where(blk == n_used - 1, block_e, 0))
    block_e = jnp.where(used, block_e, last_e)
    onehot = block_e[:, None] == experts[None, :]
    first_row = blk * MOE_BLOCK - per_block(onehot, pstart)
    n_valid = jnp.where(used, jnp.clip(per_block(onehot, counts) - first_row, 0, MOE_BLOCK), 0)
    nonempty = counts > 0
    w_slot = lax.rem(jnp.cumsum(nonempty.astype(jnp.int32)) - 1, 2)
    later = nonempty[None, :] & (experts[None, :] > experts[:, None])
    next_e = jnp.min(jnp.where(later, experts[None, :], N_EXPERTS), axis=1)
    next_e = jnp.where(next_e == N_EXPERTS, -1, next_e)
    i32 = lambda v: v.astype(jnp.int32)
    pad1 = lambda v, fill: i32(jnp.concatenate([v, jnp.full((1,), fill, v.dtype)]))
    tables = (pad1(block_e, 0), pad1(i32(used & (first_row == 0)), 0),
              pad1(per_block(onehot, w_slot), 0), pad1(per_block(onehot, next_e), -1),
              i32(n_used.reshape(1)))
    in_blk = jnp.arange(MOE_BLOCK, dtype=jnp.int32)[None, :]
    sorted_pos = (per_block(onehot, start) + first_row)[:, None] + in_blk
    a = order[jnp.clip(sorted_pos, 0, n_assign - 1)]
    valid = in_blk < n_valid[:, None]
    tok = a // TOP_K
    tail = jnp.broadcast_to(TOP_K * n + in_blk, a.shape)
    slot_src = i32(jnp.where(valid, tok, 0))
    slot_dst = jnp.where(valid, (a % TOP_K) * n + tok, tail)
    slot_dst = i32(jnp.concatenate([tail[:1], slot_dst]))
    return tables, slot_src, slot_dst


def _combined(x_ref, sel_ref, y0_ref, y1_ref):
    sel = sel_ref[...]
    y0, y1 = _unpack_halves(y0_ref[...]), _unpack_halves(y1_ref[...])
    return x_ref[...] + (y0 * sel[:, 0:1] + y1 * sel[:, 1:2])


def _pending_specs(tm, n, d):
    tiles = n // tm
    return [pl.BlockSpec((tm, d), lambda i: (i, 0)),
            pl.BlockSpec((tm, LANES), lambda i: (i, 0)),
            pl.BlockSpec((tm, d // 2), lambda i: (i, 0)),
            pl.BlockSpec((tm, d // 2), lambda i: (i + tiles, 0))]


def _combine_kernel(x_ref, sel_ref, y0_ref, y1_ref, o_ref):
    o_ref[...] = _combined(x_ref, sel_ref, y0_ref, y1_ref)


def _combine(pending):
    x2d, sel, buf = pending
    n, d = x2d.shape
    tm = ROW_TILE
    return pl.pallas_call(
        _combine_kernel,
        out_shape=jax.ShapeDtypeStruct((n, d), F32),
        grid=(n // tm,),
        in_specs=_pending_specs(tm, n, d),
        out_specs=pl.BlockSpec((tm, d), lambda i: (i, 0)),
        compiler_params=_params("parallel"),
        name="moe_combine",
    )(x2d, sel, buf, buf)


def _moe(x2d, g, w_rg, b_rg, w_re, b_re, w_gate, w_up, w_down, layer):
    n, d = x2d.shape
    pad = LANES - N_EXPERT_GROUPS - N_EXPERTS
    w_r = jnp.concatenate([w_rg, w_re, jnp.zeros((d, pad), F32)], axis=1).astype(BF16)
    b_r = jnp.concatenate([b_rg, b_re, jnp.zeros((pad,), F32)]).reshape(1, LANES)
    sel, hp = _route(x2d, g.reshape(1, d), w_r, b_r)
    tables, slot_src, slot_dst = _dispatch_plan(sel, n)
    buf = _expert_mlp(hp, w_gate, w_up, w_down, layer, tables, slot_src, slot_dst)
    return x2d, sel, buf


def _fourier_in_kernel(x_ref, sel_ref, y0_ref, y1_ref, g_ref, w_ref, ccs_ref, nyw_ref,
                       x_out_ref, a_ref, b_ref, ny_ref):
    x = _combined(x_ref, sel_ref, y0_ref, y1_ref)
    x_out_ref[...] = x
    h = _rms(x, g_ref[...]).astype(BF16)
    u = jnp.dot(h, w_ref[...], preferred_element_type=F32).astype(BF16)
    gd = ccs_ref.shape[0]
    half = gd // 2
    for gi in range(FOURIER_GROUPS):
        ab = jnp.dot(u[:, gi * gd:(gi + 1) * gd], ccs_ref[...], preferred_element_type=F32).astype(BF16)
        a_ref[:, gi * half:(gi + 1) * half] = ab[:, :half]
        b_ref[:, gi * half:(gi + 1) * half] = ab[:, half:]
    ny_ref[...] = jnp.dot(u, nyw_ref[...], preferred_element_type=F32).astype(BF16)


def _fourier_in(pending, g, w_in, ccs, nyw):
    x2d, sel, buf = pending
    n, d = x2d.shape
    tm = ROW_TILE // 2
    gd = d // FOURIER_GROUPS
    half = jax.ShapeDtypeStruct((n, d // 2), BF16)
    return pl.pallas_call(
        _fourier_in_kernel,
        out_shape=(jax.ShapeDtypeStruct((n, d), F32), half, half, jax.ShapeDtypeStruct((n, LANES), BF16)),
        grid=(n // tm,),
        in_specs=_pending_specs(tm, n, d)
                 + [pl.BlockSpec((1, d), lambda i: (0, 0)),
                    pl.BlockSpec((d, d), lambda i: (0, 0)),
                    pl.BlockSpec((gd, gd), lambda i: (0, 0)),
                    pl.BlockSpec((d, LANES), lambda i: (0, 0))],
        out_specs=(pl.BlockSpec((tm, d), lambda i: (i, 0)),
                   pl.BlockSpec((tm, d // 2), lambda i: (i, 0)),
                   pl.BlockSpec((tm, d // 2), lambda i: (i, 0)),
                   pl.BlockSpec((tm, LANES), lambda i: (i, 0))),
        compiler_params=_params("parallel"),
        name="fourier_in",
    )(x2d, sel, buf, buf, g, w_in, ccs, nyw)


def _seq_dft_kernel(cs_ref, nss_ref, a_ref, b_ref, ny_ref, fd_ref, fs_ref, fn_ref):
    p = jnp.dot(cs_ref[...], a_ref[...], preferred_element_type=F32)
    qn = jnp.dot(nss_ref[...], b_ref[...], preferred_element_type=F32)
    fd_ref[...] = (p + qn).astype(BF16)
    fs_ref[...] = (p - qn).astype(BF16)

    @pl.when(pl.program_id(2) == 0)
    def _():
        fn_ref[...] = jnp.dot(cs_ref[...], ny_ref[...], preferred_element_type=F32).astype(BF16)


def _seq_dft(cs, nss, a3, b3, ny3):
    batch, seq, dh = a3.shape
    tm = min(seq, 1024)
    tn = min(dh, 512)
    half = jax.ShapeDtypeStruct((batch, seq, dh), BF16)
    col = pl.BlockSpec((None, seq, tn), lambda b, i, j: (b, 0, j))
    out = pl.BlockSpec((None, tm, tn), lambda b, i, j: (b, i, j))
    return pl.pallas_call(
        _seq_dft_kernel,
        out_shape=(half, half, jax.ShapeDtypeStruct((batch, seq, LANES), BF16)),
        grid=(batch, seq // tm, dh // tn),
        in_specs=[pl.BlockSpec((tm, seq), lambda b, i, j: (i, 0)),
                  pl.BlockSpec((tm, seq), lambda b, i, j: (i, 0)),
                  col, col,
                  pl.BlockSpec((None, seq, LANES), lambda b, i, j: (b, 0, 0))],
        out_specs=(out, out, pl.BlockSpec((None, tm, LANES), lambda b, i, j: (b, i, 0))),
        compiler_params=_params("parallel", "parallel", "arbitrary"),
        name="seq_dft",
    )(cs, nss, a3, b3, ny3)


def _proj_residual_kernel(*refs):
    m = (len(refs) - 2) // 2
    x_ref, o_ref = refs[2 * m], refs[2 * m + 1]
    acc = x_ref[...]
    for f_ref, w_ref in zip(refs[:m], refs[m:2 * m]):
        acc = acc + jnp.dot(f_ref[...], w_ref[...], preferred_element_type=F32)
    o_ref[...] = acc


def _proj_residual(fs, ws, x2d):
    n, d = x2d.shape
    tm = ROW_TILE
    return pl.pallas_call(
        _proj_residual_kernel,
        out_shape=jax.ShapeDtypeStruct((n, d), F32),
        grid=(n // tm,),
        in_specs=[pl.BlockSpec((tm, f.shape[1]), lambda i: (i, 0)) for f in fs]
                 + [pl.BlockSpec(w.shape, lambda i: (0, 0)) for w in ws]
                 + [pl.BlockSpec((tm, d), lambda i: (i, 0))],
        out_specs=pl.BlockSpec((tm, d), lambda i: (i, 0)),
        compiler_params=_params("parallel"),
        name="proj_residual",
    )(*fs, *ws, x2d)


def _rope_tables(seq):
    half = ROT_DIM // 2
    inv_freq = ROPE_THETA ** (-np.arange(0, ROT_DIM, 2, dtype=np.float64) / ROT_DIM)
    ang = np.arange(seq, dtype=np.float64)[:, None] * inv_freq[None, :]
    cos_t = np.ones((seq, LANES))
    sa = np.zeros((seq, LANES))
    sb = np.zeros((seq, LANES))
    cos_t[:, :half] = np.cos(ang)
    cos_t[:, half:ROT_DIM] = np.cos(ang)
    sa[:, half:ROT_DIM] = np.sin(ang)
    sb[:, :half] = -np.sin(ang)
    return [jnp.asarray(t, F32) for t in (cos_t, sa, sb)]


def _dft_tables(n):
    jk = (np.arange(n, dtype=np.int64)[:, None] * np.arange(n, dtype=np.int64)[None, :]) % n
    ang = 2.0 * np.pi * jk.astype(np.float64) / n
    scale = n ** -0.5
    return np.cos(ang) * scale, np.sin(ang) * scale


def _attention_mixer(x2d, batch, seq, norm_g, w_qkv, q_g, k_g, w_out):
    n, d = x2d.shape
    width = N_ATTN_GROUPS * GROUP_WIDTH
    gw = GROUP_WIDTH
    w3 = jnp.stack([jnp.concatenate([w_qkv[:, t * width + gi * gw:t * width + (gi + 1) * gw]
                                     for t in range(3)], axis=1)
                    for gi in range(N_ATTN_GROUPS)]).astype(BF16)
    qg = (q_g * HEAD_DIM ** -0.5).reshape(1, HEAD_DIM)
    kg = k_g.reshape(1, HEAD_DIM)
    cos_t, sa_t, sb_t = _rope_tables(seq)
    qkv = _qkv_project(x2d, norm_g.reshape(1, d), w3, qg, kg, cos_t, sa_t, sb_t, batch, seq)
    outs = [_attention_group(qkv[gi], gi, batch, seq) for gi in range(N_ATTN_GROUPS)]
    return _attn_out_project([o for o, _ in outs], [l for _, l in outs], x2d, w_out.astype(BF16), seq)


def _fourier_mixer(pending, batch, seq, norm_g, w_in, w_out):
    n, d = pending[0].shape
    groups = FOURIER_GROUPS
    gd = d // groups
    half = gd // 2
    cc, sc = _dft_tables(gd)
    cs, ss = _dft_tables(seq)
    ccs = np.concatenate([cc[:, :half], sc[:, :half]], axis=1)
    nyw = np.zeros((d, LANES))
    for gi in range(groups):
        nyw[gi * gd:(gi + 1) * gd, gi] = cc[:, half]
    to_bf16 = lambda t: jnp.asarray(t, F32).astype(BF16)
    x2d, a, b, ny = _fourier_in(pending, norm_g.reshape(1, d), w_in.astype(BF16), to_bf16(ccs), to_bf16(nyw))
    fd, fs, fn = _seq_dft(to_bf16(cs), to_bf16(-ss), a.reshape(batch, seq, d // 2), b.reshape(batch, seq, d // 2),
                          ny.reshape(batch, seq, LANES))
    k = np.arange(half)
    base = (np.arange(groups) * gd)[:, None]
    rows_d = (base + k[None, :]).reshape(-1)
    rows_s = (base + (gd - k[None, :]) % gd).reshape(-1)
    keep_s = np.tile(k > 0, groups)
    w_d = w_out[rows_d].astype(BF16)
    w_s = jnp.where(jnp.asarray(keep_s)[:, None], w_out[rows_s], 0.0).astype(BF16)
    w_n = jnp.concatenate([w_out[base[:, 0] + half], jnp.zeros((LANES - groups, d), w_out.dtype)]).astype(BF16)
    flat = lambda t: t.reshape(n, t.shape[-1])
    return _proj_residual([flat(fd), flat(fs), flat(fn)], [w_d, w_s, w_n], x2d)


def kernel(x, attn_norm_g, w_qkv, q_norm_g, k_norm_g, w_attn_out, fourier_norm_g, w_fourier_in, w_fourier_out, moe_norm_g, w_router_group, b_router_group, w_router_expert, b_router_expert, w_expert_gate, w_expert_up, w_expert_down):
    batch, seq, d = x.shape
    depth = moe_norm_g.shape[0]
    h = x.reshape(batch * seq, d)
    pending = None
    for i in range(depth):
        j = i // 2
        if i % 2 == 0:
            if pending is not None:
                h = _combine(pending)
            h = _attention_mixer(h, batch, seq, attn_norm_g[j], w_qkv[j], q_norm_g[j], k_norm_g[j], w_attn_out[j])
        else:
            h = _fourier_mixer(pending, batch, seq, fourier_norm_g[j], w_fourier_in[j], w_fourier_out[j])
        pending = _moe(h, moe_norm_g[i], w_router_group[i], b_router_group[i], w_router_expert[i],
                       b_router_expert[i], w_expert_gate, w_expert_up, w_expert_down, i)
    return _combine(pending).reshape(batch, seq, d)
```

```python
import functools

import numpy as np
import jax
import jax.numpy as jnp
from jax import lax
from jax.experimental import pallas as pl
from jax.experimental.pallas import tpu as pltpu

F32 = jnp.float32
BF16 = jnp.bfloat16

HEAD_DIM = 128
HEADS_PER_GROUP = 4
DILATED_PATTERNS = ((128, 1), (512, 4), (2048, 16))
N_ATTN_GROUPS = len(DILATED_PATTERNS)
GROUP_WIDTH = HEADS_PER_GROUP * HEAD_DIM
ROT_DIM = HEAD_DIM // 4
ROPE_THETA = 500000.0
NEG_INF = -1e30
FOURIER_GROUPS = 8
N_EXPERT_GROUPS = 8
EXPERTS_PER_GROUP = 8
N_EXPERTS = N_EXPERT_GROUPS * EXPERTS_PER_GROUP
TOP_K = 2
MOE_BLOCK = 128
GATHER_AHEAD = 8
GATHER_SLOTS = GATHER_AHEAD + 1
SCATTER_SLOTS = 4
EPS = 1e-6

LANES = 128
ATTN_Q_BLOCK = 128
ATTN_TILES_PER_ROUND = 2
ROW_TILE = 512
VMEM_LIMIT = 48 * 1024 * 1024
QKV_VMEM_LIMIT = 56 * 1024 * 1024


def _params(*sem):
    return pltpu.CompilerParams(dimension_semantics=sem, vmem_limit_bytes=VMEM_LIMIT)


def _rms(x, g):
    ms = jnp.mean(x * x, axis=-1, keepdims=True)
    return x * lax.rsqrt(ms + EPS) * g


def _qkv_kernel(x_ref, g_ref, w_ref, qg_ref, kg_ref, cos_ref, sa_ref, sb_ref, *rest, n_steps):
    o_refs, (h_ref, r_new, r_old, c_ref) = rest[:N_ATTN_GROUPS], rest[N_ATTN_GROUPS:]
    s = pl.program_id(0)
    chunks, tm, _ = c_ref.shape

    @pl.when((s < n_steps) & (lax.rem(s, N_ATTN_GROUPS) == 0))
    def _():
        h_ref[...] = _rms(x_ref[...], g_ref[...]).astype(BF16)

    def project():
        w = w_ref[lax.rem(s, N_ATTN_GROUPS)]
        r_new[...] = jnp.dot(h_ref[...], w, preferred_element_type=F32)

    def finish(gi):
        cos, sa, sb = cos_ref[...], sa_ref[...], sb_ref[...]
        for j in range(2 * HEADS_PER_GROUP):
            t = r_old[:, j * HEAD_DIM:(j + 1) * HEAD_DIM]
            t = _rms(t, qg_ref[...] if j < HEADS_PER_GROUP else kg_ref[...])
            t = t * cos + pltpu.roll(t, ROT_DIM // 2, 1) * sa + pltpu.roll(t, HEAD_DIM - ROT_DIM // 2, 1) * sb
            c_ref[j] = t
        for j in range(2 * HEADS_PER_GROUP, 3 * HEADS_PER_GROUP):
            c_ref[j] = r_old[:, j * HEAD_DIM:(j + 1) * HEAD_DIM]
        dilation = DILATED_PATTERNS[gi][1]
        for phase in range(dilation):
            for c in range(chunks):
                rows = c_ref[c, pl.ds(phase, tm // dilation, stride=dilation), :]
                col = (phase * chunks + c) * LANES
                o_refs[gi][:, col:col + LANES] = rows.astype(BF16)

    def hand_over():
        r_old[...] = r_new[...]

    @pl.when(s == 0)
    def _():
        project()
        hand_over()

    for gi in range(N_ATTN_GROUPS):
        prev_is_gi = lax.rem(s + N_ATTN_GROUPS - 1, N_ATTN_GROUPS) == gi

        @pl.when((s >= 1) & (s < n_steps) & prev_is_gi)
        def _(gi=gi):
            project()
            finish(gi)
            hand_over()

    pl.when(s == n_steps)(functools.partial(finish, (n_steps - 1) % N_ATTN_GROUPS))


def _qkv_project(x2d, g, w3, qg, kg, cos_t, sa_t, sb_t, batch, seq):
    n, d = x2d.shape
    tm = ROW_TILE
    seq_tiles = seq // tm
    gw3 = 3 * GROUP_WIDTH
    ng = N_ATTN_GROUPS
    n_steps = (n // tm) * ng
    cur = lambda s: jnp.minimum(s, n_steps - 1)
    prev_tile = lambda s: jnp.maximum(s - 1, 0) // ng
    tab = pl.BlockSpec((tm, LANES), lambda s: (prev_tile(s) % seq_tiles, 0))
    dils = [dil for _, dil in DILATED_PATTERNS]

    def out_spec(gi, dil):
        def index(s):
            t = jnp.maximum(s - 1 - gi, 0) // ng
            return (t // seq_tiles, t % seq_tiles, 0)
        return pl.BlockSpec((None, tm // dil, dil * gw3), index)

    return pl.pallas_call(
        functools.partial(_qkv_kernel, n_steps=n_steps),
        out_shape=[jax.ShapeDtypeStruct((batch, seq // dil, dil * gw3), BF16) for dil in dils],
        grid=(n_steps + 1,),
        in_specs=[
            pl.BlockSpec((tm, d), lambda s: (cur(s) // ng, 0)),
            pl.BlockSpec((1, d), lambda s: (0, 0)),
            pl.BlockSpec((ng, d, gw3), lambda s: (0, 0, 0), pipeline_mode=pl.Buffered(1)),
            pl.BlockSpec((1, LANES), lambda s: (0, 0)),
            pl.BlockSpec((1, LANES), lambda s: (0, 0)),
            tab, tab, tab,
        ],
        out_specs=[out_spec(gi, dil) for gi, dil in enumerate(dils)],
        scratch_shapes=[pltpu.VMEM((tm, d), BF16), pltpu.VMEM((tm, gw3), F32), pltpu.VMEM((tm, gw3), F32),
                        pltpu.VMEM((gw3 // LANES, tm, LANES), F32)],
        compiler_params=pltpu.CompilerParams(dimension_semantics=("arbitrary",),
                                             vmem_limit_bytes=QKV_VMEM_LIMIT),
        name="qkv_project",
    )(x2d, g, w3, qg, kg, cos_t, sa_t, sb_t)


def _attn_kernel(qkv_ref, o_ref, lse_ref, *, dilation, length, radius):
    gw = GROUP_WIDTH
    bq = ATTN_Q_BLOCK
    win = min(length, bq + 2 * radius)
    lane = lax.broadcasted_iota(jnp.int32, (bq, LANES), 1)
    tiles = [(r, qb) for r in range(dilation) for qb in range(length // bq)]
    for t0 in range(0, len(tiles), ATTN_TILES_PER_ROUND):
        work = []
        for r, qb in tiles[t0:t0 + ATTN_TILES_PER_ROUND]:
            q0 = qb * bq
            k0 = min(max(q0 - radius, 0), length - win)
            jq = q0 + lax.broadcasted_iota(jnp.int32, (bq, win), 0)
            jk = k0 + lax.broadcasted_iota(jnp.int32, (bq, win), 1)
            valid = jnp.abs(jk - jq) <= radius
            for hh in range(HEADS_PER_GROUP):
                c = r * 3 * gw + hh * HEAD_DIM
                q = qkv_ref[q0:q0 + bq, c:c + HEAD_DIM]
                k = qkv_ref[k0:k0 + win, c + gw:c + gw + HEAD_DIM]
                s = lax.dot_general(q, k, (((1,), (1,)), ((), ())), preferred_element_type=F32)
                work.append((r, q0, k0, hh, jnp.where(valid, s, NEG_INF)))
        soft = []
        for r, q0, k0, hh, s in work:
            m = jnp.max(s, axis=-1, keepdims=True)
            p = jnp.exp(s - m)
            l = jnp.sum(p, axis=-1, keepdims=True)
            soft.append((p.astype(BF16), l, m + jnp.log(l)))
        lse_tile = None
        for (r, q0, k0, hh, _), (p, l, lse) in zip(work, soft):
            c = r * 3 * gw + hh * HEAD_DIM
            v = qkv_ref[k0:k0 + win, c + 2 * gw:c + 2 * gw + HEAD_DIM]
            o = jnp.dot(p, v, preferred_element_type=F32) / l
            oc = r * gw + hh * HEAD_DIM
            o_ref[q0:q0 + bq, oc:oc + HEAD_DIM] = o.astype(BF16)
            lse_tile = jnp.where(lane == hh, lse, jnp.zeros((bq, LANES), F32) if hh == 0 else lse_tile)
            if hh == HEADS_PER_GROUP - 1:
                lse_ref[q0:q0 + bq, r * LANES:(r + 1) * LANES] = lse_tile


def _attention_group(qkv_g, gi, batch, seq):
    window, dilation = DILATED_PATTERNS[gi]
    radius = (window // 2) // dilation
    length = seq // dilation
    gw3 = 3 * GROUP_WIDTH
    kern = functools.partial(_attn_kernel, dilation=dilation, length=length, radius=radius)
    return pl.pallas_call(
        kern,
        out_shape=(jax.ShapeDtypeStruct((batch, length, dilation * GROUP_WIDTH), BF16),
                   jax.ShapeDtypeStruct((batch, length, dilation * LANES), F32)),
        grid=(batch,),
        in_specs=[pl.BlockSpec((None, length, dilation * gw3), lambda b: (b, 0, 0))],
        out_specs=(pl.BlockSpec((None, length, dilation * GROUP_WIDTH), lambda b: (b, 0, 0)),
                   pl.BlockSpec((None, length, dilation * LANES), lambda b: (b, 0, 0))),
        compiler_params=_params("parallel"),
        name=f"band_attention_d{dilation}",
    )(qkv_g)


def _attn_out_kernel(o0, o1, o2, l0, l1, l2, x_ref, w_ref, out_ref, mix_ref, o_rows, l_rows):
    tm = out_ref.shape[0]
    for gi, (o_ref, l_ref) in enumerate(((o0, l0), (o1, l1), (o2, l2))):
        dilation = DILATED_PATTERNS[gi][1]
        for phase in range(dilation):
            dst = pl.ds(phase, tm // dilation, stride=dilation)
            for hh in range(HEADS_PER_GROUP):
                col = phase * GROUP_WIDTH + hh * HEAD_DIM
                o_rows[gi * HEADS_PER_GROUP + hh, dst, :] = o_ref[:, col:col + HEAD_DIM].astype(F32)
            l_rows[gi, dst, :] = l_ref[:, phase * LANES:(phase + 1) * LANES]
    ls = [l_rows[gi] for gi in range(N_ATTN_GROUPS)]
    m = jnp.maximum(jnp.maximum(ls[0], ls[1]), ls[2])
    es = [jnp.exp(l - m) for l in ls]
    den = es[0] + es[1] + es[2]
    for gi in range(N_ATTN_GROUPS):
        alpha = es[gi] / den
        for hh in range(HEADS_PER_GROUP):
            c = hh * HEAD_DIM
            a = alpha[:, hh:hh + 1]
            mix_ref[:, gi * GROUP_WIDTH + c:gi * GROUP_WIDTH + c + HEAD_DIM] = (
                o_rows[gi * HEADS_PER_GROUP + hh] * a).astype(BF16)
    out_ref[...] = x_ref[...] + jnp.dot(mix_ref[...], w_ref[...], preferred_element_type=F32)


def _attn_out_project(os_, lses, x2d, w_out, seq):
    n, d = x2d.shape
    tm = ROW_TILE
    seq_tiles = seq // tm
    width = N_ATTN_GROUPS * GROUP_WIDTH
    dils = [dil for _, dil in DILATED_PATTERNS]
    pm = lambda i: (i // seq_tiles, i % seq_tiles, 0)
    return pl.pallas_call(
        _attn_out_kernel,
        out_shape=jax.ShapeDtypeStruct((n, d), F32),
        grid=(n // tm,),
        in_specs=[pl.BlockSpec((None, tm // dil, dil * GROUP_WIDTH), pm) for dil in dils]
                 + [pl.BlockSpec((None, tm // dil, dil * LANES), pm) for dil in dils]
                 + [pl.BlockSpec((tm, d), lambda i: (i, 0)),
                    pl.BlockSpec((width, d), lambda i: (0, 0))],
        out_specs=pl.BlockSpec((tm, d), lambda i: (i, 0)),
        scratch_shapes=[pltpu.VMEM((tm, width), BF16),
                        pltpu.VMEM((N_ATTN_GROUPS * HEADS_PER_GROUP, tm, HEAD_DIM), F32),
                        pltpu.VMEM((N_ATTN_GROUPS, tm, LANES), F32)],
        compiler_params=_params("parallel"),
        name="attn_out_project",
    )(*os_, *lses, x2d, w_out)


U32 = jnp.uint32
HIGH_HALF = np.uint32(0xFFFF0000)


def _pack_halves(x):
    c = x.shape[1] // 2
    lo = lax.bitcast_convert_type(x[:, :c].astype(BF16).astype(F32), U32)
    hi = lax.bitcast_convert_type(x[:, c:].astype(BF16).astype(F32), U32)
    return (lo >> 16) | (hi & HIGH_HALF)


def _unpack_halves(u):
    lo = lax.bitcast_convert_type(u << 16, F32)
    hi = lax.bitcast_convert_type(u & HIGH_HALF, F32)
    return jnp.concatenate([lo, hi], axis=1)


def _router_kernel(x_ref, g_ref, w_ref, b_ref, sel_ref, hp_ref):
    hf = _rms(x_ref[...], g_ref[...])
    hp_ref[...] = _pack_halves(hf)
    h = hf.astype(BF16)
    logits = jnp.dot(h, w_ref[...], preferred_element_type=F32) + b_ref[...]
    lane = lax.broadcasted_iota(jnp.int32, logits.shape, 1)
    lanef = lane.astype(F32)
    big = float(LANES)
    is_grp = lane < N_EXPERT_GROUPS
    coarse = jnp.where(is_grp, logits, -jnp.inf)
    cmax = jnp.max(coarse, axis=-1, keepdims=True)
    g_sel = jnp.min(jnp.where(coarse == cmax, lanef, big), axis=-1, keepdims=True)
    den = jnp.sum(jnp.where(is_grp, jnp.exp(logits - cmax), 0.0), axis=-1, keepdims=True)
    g_gate = 1.0 / den
    lo = N_EXPERT_GROUPS + g_sel * EXPERTS_PER_GROUP
    in_grp = (lanef >= lo) & (lanef < lo + EXPERTS_PER_GROUP)
    fine = jnp.where(in_grp, logits, -jnp.inf)
    v1 = jnp.max(fine, axis=-1, keepdims=True)
    i1 = jnp.min(jnp.where(fine == v1, lanef, big), axis=-1, keepdims=True)
    fine2 = jnp.where(lanef == i1, -jnp.inf, fine)
    v2 = jnp.max(fine2, axis=-1, keepdims=True)
    i2 = jnp.min(jnp.where(fine2 == v2, lanef, big), axis=-1, keepdims=True)
    e2 = jnp.exp(v2 - v1)
    w1 = g_gate * (1.0 / (1.0 + e2))
    w2 = g_gate * (e2 / (1.0 + e2))
    sel = jnp.where(lane == 0, w1, 0.0)
    sel = jnp.where(lane == 1, w2, sel)
    sel = jnp.where(lane == 2, i1 - N_EXPERT_GROUPS, sel)
    sel = jnp.where(lane == 3, i2 - N_EXPERT_GROUPS, sel)
    sel_ref[...] = sel


def _route(x2d, g, w_r, b_r):
    n, d = x2d.shape
    tm = ROW_TILE
    return pl.pallas_call(
        _router_kernel,
        out_shape=(jax.ShapeDtypeStruct((n, LANES), F32), jax.ShapeDtypeStruct((n, d // 2), U32)),
        grid=(n // tm,),
        in_specs=[pl.BlockSpec((tm, d), lambda i: (i, 0)),
                  pl.BlockSpec((1, d), lambda i: (0, 0)),
                  pl.BlockSpec((d, LANES), lambda i: (0, 0)),
                  pl.BlockSpec((1, LANES), lambda i: (0, 0))],
        out_specs=(pl.BlockSpec((tm, LANES), lambda i: (i, 0)), pl.BlockSpec((tm, d // 2), lambda i: (i, 0))),
        compiler_params=_params("parallel"),
        name="moe_router",
    )(x2d, g, w_r, b_r)


def _expert_kernel(be_ref, first_ref, ws_ref, ne_ref, nu_ref,
                   src_ref, dst_ref,
                   h_hbm, wg_hbm, wu_hbm, wd_hbm,
                   buf_hbm,
                   xbuf, ybuf, wgb, wub, wdb, gsem, ssem, wsem, *, layer):
    i = pl.program_id(0)
    n_used = nu_ref[0]
    slot = lax.rem(i, SCATTER_SLOTS)
    prev_slot = lax.rem(i + SCATTER_SLOTS - 1, SCATTER_SLOTS)
    gslot = lax.rem(i, GATHER_SLOTS)
    rows = MOE_BLOCK

    last_block = src_ref.shape[0] - 1

    def start_gather(block, s):
        block = jnp.minimum(block, last_block)
        for r in range(rows):
            pltpu.make_async_copy(h_hbm.at[pl.ds(src_ref[block, r], 1)], xbuf.at[s, pl.ds(r, 1)],
                                  gsem.at[s]).start()

    def wait_gather(s):
        pltpu.make_async_copy(h_hbm.at[pl.ds(0, rows)], xbuf.at[s], gsem.at[s]).wait()

    def start_scatter(s):
        for r in range(rows):
            pltpu.make_async_copy(ybuf.at[s, pl.ds(r, 1)], buf_hbm.at[pl.ds(dst_ref[i, r], 1)],
                                  ssem.at[s]).start()

    def wait_scatter(s):
        pltpu.make_async_copy(ybuf.at[s], buf_hbm.at[pl.ds(0, rows)], ssem.at[s]).wait()

    def weight_copies(e, s):
        return [(pltpu.make_async_copy(hbm.at[layer, e], vmem.at[s], wsem.at[s]), 1)
                for hbm, vmem in ((wg_hbm, wgb), (wu_hbm, wub), (wd_hbm, wdb))]

    @pl.when(i == 0)
    def _():
        ybuf[...] = jnp.zeros_like(ybuf)
        for b in range(GATHER_AHEAD):
            start_gather(b, b)
        for c, q in weight_copies(be_ref[0], 0):
            c.start(priority=q)

    @pl.when(i < n_used)
    def _():
        wait_gather(gslot)

        @pl.when(i >= SCATTER_SLOTS - 1)
        def _():
            wait_scatter(slot)

        @pl.when(first_ref[i] == 1)
        def _():
            s = ws_ref[i]
            for c, _ in weight_copies(be_ref[i], s):
                c.wait()

            @pl.when(ne_ref[i] >= 0)
            def _():
                for c, q in weight_copies(ne_ref[i], 1 - s):
                    c.start(priority=q)

        start_scatter(prev_slot)
        ws = ws_ref[i]
        h = _unpack_halves(xbuf[gslot]).astype(BF16)
        gate = jnp.dot(h, wgb[ws].astype(BF16), preferred_element_type=F32)
        up = jnp.dot(h, wub[ws].astype(BF16), preferred_element_type=F32)
        act = (gate * (1.0 / (1.0 + jnp.exp(-gate))) * up).astype(BF16)
        ybuf[slot] = _pack_halves(jnp.dot(act, wdb[ws].astype(BF16), preferred_element_type=F32))
        start_gather(i + GATHER_AHEAD, lax.rem(i + GATHER_AHEAD, GATHER_SLOTS))

    @pl.when(i == n_used)
    def _():
        start_scatter(prev_slot)
        for b in range(SCATTER_SLOTS):
            @pl.when(i >= b)
            def _(b=b):
                wait_scatter(lax.rem(i + 2 * SCATTER_SLOTS - 1 - b, SCATTER_SLOTS))
        for b in range(GATHER_AHEAD):
            wait_gather(lax.rem(i + b, GATHER_SLOTS))


def _expert_mlp(hp, w_gate, w_up, w_down, layer, tables, slot_src, slot_dst):
    n, dp = hp.shape
    d_exp = w_gate.shape[-1]
    d = w_gate.shape[-2]
    rows = MOE_BLOCK
    n_blocks = slot_src.shape[0]
    grid_spec = pltpu.PrefetchScalarGridSpec(
        num_scalar_prefetch=len(tables) + 2,
        grid=(n_blocks + 1,),
        in_specs=[
            pl.BlockSpec(memory_space=pl.ANY),
            pl.BlockSpec(memory_space=pl.ANY),
            pl.BlockSpec(memory_space=pl.ANY),
            pl.BlockSpec(memory_space=pl.ANY),
        ],
        out_specs=pl.BlockSpec(memory_space=pl.ANY),
        scratch_shapes=[pltpu.VMEM((GATHER_SLOTS, rows, dp), U32), pltpu.VMEM((SCATTER_SLOTS, rows, dp), U32),
                        pltpu.VMEM((2, d, d_exp), F32), pltpu.VMEM((2, d, d_exp), F32),
                        pltpu.VMEM((2, d_exp, d), F32),
                        pltpu.SemaphoreType.DMA((GATHER_SLOTS,)), pltpu.SemaphoreType.DMA((SCATTER_SLOTS,)),
                        pltpu.SemaphoreType.DMA((2,))],
    )
    return pl.pallas_call(
        functools.partial(_expert_kernel, layer=layer),
        out_shape=jax.ShapeDtypeStruct((TOP_K * n + rows, dp), U32),
        grid_spec=grid_spec,
        compiler_params=_params("arbitrary"),
        name="moe_expert_mlp",
    )(*tables, slot_src, slot_dst, hp, w_gate, w_up, w_down)


def _dispatch_plan(sel, n):
    n_assign = n * TOP_K
    n_blocks = -(-n_assign // MOE_BLOCK) + N_EXPERTS
    e_flat = sel[:, TOP_K:2 * TOP_K].astype(jnp.int32).reshape(-1)
    experts = jnp.arange(N_EXPERTS, dtype=jnp.int32)
    counts = jnp.sum((e_flat[:, None] == experts[None, :]).astype(jnp.int32), axis=0)
    padded = ((counts + MOE_BLOCK - 1) // MOE_BLOCK) * MOE_BLOCK
    pend = jnp.cumsum(padded)
    pstart = pend - padded
    start = jnp.cumsum(counts) - counts
    order = jnp.argsort(e_flat, stable=True).astype(jnp.int32)
    n_used = pend[-1] // MOE_BLOCK
    blk = jnp.arange(n_blocks, dtype=jnp.int32)
    used = blk < n_used

    def per_block(onehot, table):
        return jnp.sum(jnp.where(onehot, table[None, :], 0), axis=1)

    block_e = jnp.minimum(jnp.sum((pend[None, :] <= (blk * MOE_BLOCK)[:, None]).astype(jnp.int32), axis=1),
                          N_EXPERTS - 1)
    last_e = jnp.sum(jnp.where(blk == n_used - 1, block_e, 0))
    block_e = jnp.where(used, block_e, last_e)
    onehot = block_e[:, None] == experts[None, :]
    first_row = blk * MOE_BLOCK - per_block(onehot, pstart)
    n_valid = jnp.where(used, jnp.clip(per_block(onehot, counts) - first_row, 0, MOE_BLOCK), 0)
    nonempty = counts > 0
    w_slot = lax.rem(jnp.cumsum(nonempty.astype(jnp.int32)) - 1, 2)
    later = nonempty[None, :] & (experts[None, :] > experts[:, None])
    next_e = jnp.min(jnp.where(later, experts[None, :], N_EXPERTS), axis=1)
    next_e = jnp.where(next_e == N_EXPERTS, -1, next_e)
    i32 = lambda v: v.astype(jnp.int32)
    pad1 = lambda v, fill: i32(jnp.concatenate([v, jnp.full((1,), fill, v.dtype)]))
    tables = (pad1(block_e, 0), pad1(i32(used & (first_row == 0)), 0),
              pad1(per_block(onehot, w_slot), 0), pad1(per_block(onehot, next_e), -1),
              i32(n_used.reshape(1)))
    in_blk = jnp.arange(MOE_BLOCK, dtype=jnp.int32)[None, :]
    sorted_pos = (per_block(onehot, start) + first_row)[:, None] + in_blk
    a = order[jnp.clip(sorted_pos, 0, n_assign - 1)]
    valid = in_blk < n_valid[:, None]
    tok = a // TOP_K
    tail = jnp.broadcast_to(TOP_K * n + in_blk, a.shape)
    slot_src = i32(jnp.where(valid, tok, 0))
    slot_dst = jnp.where(valid, (a % TOP_K) * n + tok, tail)
    slot_dst = i32(jnp.concatenate([tail[:1], slot_dst]))
    return tables, slot_src, slot_dst


def _combined(x_ref, sel_ref, y0_ref, y1_ref):
    sel = sel_ref[...]
    y0, y1 = _unpack_halves(y0_ref[...]), _unpack_halves(y1_ref[...])
    return x_ref[...] + (y0 * sel[:, 0:1] + y1 * sel[:, 1:2])


def _pending_specs(tm, n, d):
    tiles = n // tm
    return [pl.BlockSpec((tm, d), lambda i: (i, 0)),
            pl.BlockSpec((tm, LANES), lambda i: (i, 0)),
            pl.BlockSpec((tm, d // 2), lambda i: (i, 0)),
            pl.BlockSpec((tm, d // 2), lambda i: (i + tiles, 0))]


def _combine_kernel(x_ref, sel_ref, y0_ref, y1_ref, o_ref):
    o_ref[...] = _combined(x_ref, sel_ref, y0_ref, y1_ref)


def _combine(pending):
    x2d, sel, buf = pending
    n, d = x2d.shape
    tm = ROW_TILE
    return pl.pallas_call(
        _combine_kernel,
        out_shape=jax.ShapeDtypeStruct((n, d), F32),
        grid=(n // tm,),
        in_specs=_pending_specs(tm, n, d),
        out_specs=pl.BlockSpec((tm, d), lambda i: (i, 0)),
        compiler_params=_params("parallel"),
        name="moe_combine",
    )(x2d, sel, buf, buf)


def _moe(x2d, g, w_rg, b_rg, w_re, b_re, w_gate, w_up, w_down, layer):
    n, d = x2d.shape
    pad = LANES - N_EXPERT_GROUPS - N_EXPERTS
    w_r = jnp.concatenate([w_rg, w_re, jnp.zeros((d, pad), F32)], axis=1).astype(BF16)
    b_r = jnp.concatenate([b_rg, b_re, jnp.zeros((pad,), F32)]).reshape(1, LANES)
    sel, hp = _route(x2d, g.reshape(1, d), w_r, b_r)
    tables, slot_src, slot_dst = _dispatch_plan(sel, n)
    buf = _expert_mlp(hp, w_gate, w_up, w_down, layer, tables, slot_src, slot_dst)
    return x2d, sel, buf


def _fourier_in_kernel(x_ref, sel_ref, y0_ref, y1_ref, g_ref, w_ref, ccs_ref, nyw_ref,
                       x_out_ref, a_ref, b_ref, ny_ref):
    x = _combined(x_ref, sel_ref, y0_ref, y1_ref)
    x_out_ref[...] = x
    h = _rms(x, g_ref[...]).astype(BF16)
    u = jnp.dot(h, w_ref[...], preferred_element_type=F32).astype(BF16)
    gd = ccs_ref.shape[0]
    half = gd // 2
    for gi in range(FOURIER_GROUPS):
        ab = jnp.dot(u[:, gi * gd:(gi + 1) * gd], ccs_ref[...], preferred_element_type=F32).astype(BF16)
        a_ref[:, gi * half:(gi + 1) * half] = ab[:, :half]
        b_ref[:, gi * half:(gi + 1) * half] = ab[:, half:]
    ny_ref[...] = jnp.dot(u, nyw_ref[...], preferred_element_type=F32).astype(BF16)


def _fourier_in(pending, g, w_in, ccs, nyw):
    x2d, sel, buf = pending
    n, d = x2d.shape
    tm = ROW_TILE // 2
    gd = d // FOURIER_GROUPS
    half = jax.ShapeDtypeStruct((n, d // 2), BF16)
    return pl.pallas_call(
        _fourier_in_kernel,
        out_shape=(jax.ShapeDtypeStruct((n, d), F32), half, half, jax.ShapeDtypeStruct((n, LANES), BF16)),
        grid=(n // tm,),
        in_specs=_pending_specs(tm, n, d)
                 + [pl.BlockSpec((1, d), lambda i: (0, 0)),
                    pl.BlockSpec((d, d), lambda i: (0, 0)),
                    pl.BlockSpec((gd, gd), lambda i: (0, 0)),
                    pl.BlockSpec((d, LANES), lambda i: (0, 0))],
        out_specs=(pl.BlockSpec((tm, d), lambda i: (i, 0)),
                   pl.BlockSpec((tm, d // 2), lambda i: (i, 0)),
                   pl.BlockSpec((tm, d // 2), lambda i: (i, 0)),
                   pl.BlockSpec((tm, LANES), lambda i: (i, 0))),
        compiler_params=_params("parallel"),
        name="fourier_in",
    )(x2d, sel, buf, buf, g, w_in, ccs, nyw)


def _seq_dft_kernel(cs_ref, nss_ref, a_ref, b_ref, ny_ref, fd_ref, fs_ref, fn_ref):
    p = jnp.dot(cs_ref[...], a_ref[...], preferred_element_type=F32)
    qn = jnp.dot(nss_ref[...], b_ref[...], preferred_element_type=F32)
    fd_ref[...] = (p + qn).astype(BF16)
    fs_ref[...] = (p - qn).astype(BF16)

    @pl.when(pl.program_id(2) == 0)
    def _():
        fn_ref[...] = jnp.dot(cs_ref[...], ny_ref[...], preferred_element_type=F32).astype(BF16)


def _seq_dft(cs, nss, a3, b3, ny3):
    batch, seq, dh = a3.shape
    tm = min(seq, 1024)
    tn = min(dh, 512)
    half = jax.ShapeDtypeStruct((batch, seq, dh), BF16)
    col = pl.BlockSpec((None, seq, tn), lambda b, i, j: (b, 0, j))
    out = pl.BlockSpec((None, tm, tn), lambda b, i, j: (b, i, j))
    return pl.pallas_call(
        _seq_dft_kernel,
        out_shape=(half, half, jax.ShapeDtypeStruct((batch, seq, LANES), BF16)),
        grid=(batch, seq // tm, dh // tn),
        in_specs=[pl.BlockSpec((tm, seq), lambda b, i, j: (i, 0)),
                  pl.BlockSpec((tm, seq), lambda b, i, j: (i, 0)),
                  col, col,
                  pl.BlockSpec((None, seq, LANES), lambda b, i, j: (b, 0, 0))],
        out_specs=(out, out, pl.BlockSpec((None, tm, LANES), lambda b, i, j: (b, i, 0))),
        compiler_params=_params("parallel", "parallel", "arbitrary"),
        name="seq_dft",
    )(cs, nss, a3, b3, ny3)


def _proj_residual_kernel(*refs):
    m = (len(refs) - 2) // 2
    x_ref, o_ref = refs[2 * m], refs[2 * m + 1]
    acc = x_ref[...]
    for f_ref, w_ref in zip(refs[:m], refs[m:2 * m]):
        acc = acc + jnp.dot(f_ref[...], w_ref[...], preferred_element_type=F32)
    o_ref[...] = acc


def _proj_residual(fs, ws, x2d):
    n, d = x2d.shape
    tm = ROW_TILE
    return pl.pallas_call(
        _proj_residual_kernel,
        out_shape=jax.ShapeDtypeStruct((n, d), F32),
        grid=(n // tm,),
        in_specs=[pl.BlockSpec((tm, f.shape[1]), lambda i: (i, 0)) for f in fs]
                 + [pl.BlockSpec(w.shape, lambda i: (0, 0)) for w in ws]
                 + [pl.BlockSpec((tm, d), lambda i: (i, 0))],
        out_specs=pl.BlockSpec((tm, d), lambda i: (i, 0)),
        compiler_params=_params("parallel"),
        name="proj_residual",
    )(*fs, *ws, x2d)


def _rope_tables(seq):
    half = ROT_DIM // 2
    inv_freq = ROPE_THETA ** (-np.arange(0, ROT_DIM, 2, dtype=np.float64) / ROT_DIM)
    ang = np.arange(seq, dtype=np.float64)[:, None] * inv_freq[None, :]
    cos_t = np.ones((seq, LANES))
    sa = np.zeros((seq, LANES))
    sb = np.zeros((seq, LANES))
    cos_t[:, :half] = np.cos(ang)
    cos_t[:, half:ROT_DIM] = np.cos(ang)
    sa[:, half:ROT_DIM] = np.sin(ang)
    sb[:, :half] = -np.sin(ang)
    return [jnp.asarray(t, F32) for t in (cos_t, sa, sb)]


def _dft_tables(n):
    jk = (np.arange(n, dtype=np.int64)[:, None] * np.arange(n, dtype=np.int64)[None, :]) % n
    ang = 2.0 * np.pi * jk.astype(np.float64) / n
    scale = n ** -0.5
    return np.cos(ang) * scale, np.sin(ang) * scale


def _attention_mixer(x2d, batch, seq, norm_g, w_qkv, q_g, k_g, w_out):
    n, d = x2d.shape
    width = N_ATTN_GROUPS * GROUP_WIDTH
    gw = GROUP_WIDTH
    w3 = jnp.stack([jnp.concatenate([w_qkv[:, t * width + gi * gw:t * width + (gi + 1) * gw]
                                     for t in range(3)], axis=1)
                    for gi in range(N_ATTN_GROUPS)]).astype(BF16)
    qg = (q_g * HEAD_DIM ** -0.5).reshape(1, HEAD_DIM)
    kg = k_g.reshape(1, HEAD_DIM)
    cos_t, sa_t, sb_t = _rope_tables(seq)
    qkv = _qkv_project(x2d, norm_g.reshape(1, d), w3, qg, kg, cos_t, sa_t, sb_t, batch, seq)
    outs = [_attention_group(qkv[gi], gi, batch, seq) for gi in range(N_ATTN_GROUPS)]
    return _attn_out_project([o for o, _ in outs], [l for _, l in outs], x2d, w_out.astype(BF16), seq)


def _fourier_mixer(pending, batch, seq, norm_g, w_in, w_out):
    n, d = pending[0].shape
    groups = FOURIER_GROUPS
    gd = d // groups
    half = gd // 2
    cc, sc = _dft_tables(gd)
    cs, ss = _dft_tables(seq)
    ccs = np.concatenate([cc[:, :half], sc[:, :half]], axis=1)
    nyw = np.zeros((d, LANES))
    for gi in range(groups):
        nyw[gi * gd:(gi + 1) * gd, gi] = cc[:, half]
    to_bf16 = lambda t: jnp.asarray(t, F32).astype(BF16)
    x2d, a, b, ny = _fourier_in(pending, norm_g.reshape(1, d), w_in.astype(BF16), to_bf16(ccs), to_bf16(nyw))
    fd, fs, fn = _seq_dft(to_bf16(cs), to_bf16(-ss), a.reshape(batch, seq, d // 2), b.reshape(batch, seq, d // 2),
                          ny.reshape(batch, seq, LANES))
    k = np.arange(half)
    base = (np.arange(groups) * gd)[:, None]
    rows_d = (base + k[None, :]).reshape(-1)
    rows_s = (base + (gd - k[None, :]) % gd).reshape(-1)
    keep_s = np.tile(k > 0, groups)
    w_d = w_out[rows_d].astype(BF16)
    w_s = jnp.where(jnp.asarray(keep_s)[:, None], w_out[rows_s], 0.0).astype(BF16)
    w_n = jnp.concatenate([w_out[base[:, 0] + half], jnp.zeros((LANES - groups, d), w_out.dtype)]).astype(BF16)
    flat = lambda t: t.reshape(n, t.shape[-1])
    return _proj_residual([flat(fd), flat(fs), flat(fn)], [w_d, w_s, w_n], x2d)


def kernel(x, attn_norm_g, w_qkv, q_norm_g, k_norm_g, w_attn_out, fourier_norm_g, w_fourier_in, w_fourier_out, moe_norm_g, w_router_group, b_router_group, w_router_expert, b_router_expert, w_expert_gate, w_expert_up, w_expert_down):
    batch, seq, d = x.shape
    depth = moe_norm_g.shape[0]
    h = x.reshape(batch * seq, d)
    pending = None
    for i in range(depth):
        j = i // 2
        if i % 2 == 0:
            if pending is not None:
                h = _combine(pending)
            h = _attention_mixer(h, batch, seq, attn_norm_g[j], w_qkv[j], q_norm_g[j], k_norm_g[j], w_attn_out[j])
        else:
            h = _fourier_mixer(pending, batch, seq, fourier_norm_g[j], w_fourier_in[j], w_fourier_out[j])
        pending = _moe(h, moe_norm_g[i], w_router_group[i], b_router_group[i], w_router_expert[i],
                       b_router_expert[i], w_expert_gate, w_expert_up, w_expert_down, i)
    return _combine(pending).reshape(batch, seq, d)
```

```python
import functools

import numpy as np
import jax
import jax.numpy as jnp
from jax import lax
from jax.experimental import pallas as pl
from jax.experimental.pallas import tpu as pltpu

F32 = jnp.float32
BF16 = jnp.bfloat16

HEAD_DIM = 128
HEADS_PER_GROUP = 4
DILATED_PATTERNS = ((128, 1), (512, 4), (2048, 16))
N_ATTN_GROUPS = len(DILATED_PATTERNS)
GROUP_WIDTH = HEADS_PER_GROUP * HEAD_DIM
ROT_DIM = HEAD_DIM // 4
ROPE_THETA = 500000.0
NEG_INF = -1e30
FOURIER_GROUPS = 8
N_EXPERT_GROUPS = 8
EXPERTS_PER_GROUP = 8
N_EXPERTS = N_EXPERT_GROUPS * EXPERTS_PER_GROUP
TOP_K = 2
MOE_BLOCK = 128
GATHER_AHEAD = 8
GATHER_SLOTS = GATHER_AHEAD + 1
SCATTER_SLOTS = 4
EPS = 1e-6

LANES = 128
ATTN_Q_BLOCK = 128
ATTN_TILES_PER_ROUND = 2
ROW_TILE = 512
VMEM_LIMIT = 48 * 1024 * 1024
QKV_VMEM_LIMIT = 56 * 1024 * 1024


def _params(*sem):
    return pltpu.CompilerParams(dimension_semantics=sem, vmem_limit_bytes=VMEM_LIMIT)


def _rms(x, g):
    ms = jnp.mean(x * x, axis=-1, keepdims=True)
    return x * lax.rsqrt(ms + EPS) * g


def _qkv_kernel(x_ref, g_ref, w_ref, qg_ref, kg_ref, cos_ref, sa_ref, sb_ref, *rest, n_steps):
    o_refs, (h_ref, h_next, r_new, r_old, c_ref) = rest[:N_ATTN_GROUPS], rest[N_ATTN_GROUPS:]
    s = pl.program_id(0)
    chunks, tm, _ = c_ref.shape

    def project():
        w = w_ref[lax.rem(s, N_ATTN_GROUPS)]
        r_new[...] = jnp.dot(h_ref[...], w, preferred_element_type=F32)

    def finish(gi):
        cos, sa, sb = cos_ref[...], sa_ref[...], sb_ref[...]
        for j in range(2 * HEADS_PER_GROUP):
            t = r_old[:, j * HEAD_DIM:(j + 1) * HEAD_DIM]
            t = _rms(t, qg_ref[...] if j < HEADS_PER_GROUP else kg_ref[...])
            t = t * cos + pltpu.roll(t, ROT_DIM // 2, 1) * sa + pltpu.roll(t, HEAD_DIM - ROT_DIM // 2, 1) * sb
            c_ref[j] = t
        for j in range(2 * HEADS_PER_GROUP, 3 * HEADS_PER_GROUP):
            c_ref[j] = r_old[:, j * HEAD_DIM:(j + 1) * HEAD_DIM]
        dilation = DILATED_PATTERNS[gi][1]
        for phase in range(dilation):
            for c in range(chunks):
                rows = c_ref[c, pl.ds(phase, tm // dilation, stride=dilation), :]
                col = (phase * chunks + c) * LANES
                o_refs[gi][:, col:col + LANES] = rows.astype(BF16)

    def hand_over():
        r_old[...] = r_new[...]

    @pl.when(s == 0)
    def _():
        h_ref[...] = _rms(x_ref[...], g_ref[...]).astype(BF16)
        project()
        hand_over()

    for gi in range(N_ATTN_GROUPS):
        prev_is_gi = lax.rem(s + N_ATTN_GROUPS - 1, N_ATTN_GROUPS) == gi

        @pl.when((s >= 1) & (s < n_steps) & prev_is_gi)
        def _(gi=gi):
            project()
            finish(gi)
            hand_over()
            if gi == N_ATTN_GROUPS - 2:
                h_next[...] = _rms(x_ref[...], g_ref[...]).astype(BF16)
                h_ref[...] = h_next[...]

    pl.when(s == n_steps)(functools.partial(finish, (n_steps - 1) % N_ATTN_GROUPS))


def _qkv_project(x2d, g, w3, qg, kg, cos_t, sa_t, sb_t, batch, seq):
    n, d = x2d.shape
    tm = ROW_TILE
    seq_tiles = seq // tm
    gw3 = 3 * GROUP_WIDTH
    ng = N_ATTN_GROUPS
    n_steps = (n // tm) * ng
    prev_tile = lambda s: jnp.maximum(s - 1, 0) // ng
    tab = pl.BlockSpec((tm, LANES), lambda s: (prev_tile(s) % seq_tiles, 0))
    dils = [dil for _, dil in DILATED_PATTERNS]

    def out_spec(gi, dil):
        def index(s):
            t = jnp.maximum(s - 1 - gi, 0) // ng
            return (t // seq_tiles, t % seq_tiles, 0)
        return pl.BlockSpec((None, tm // dil, dil * gw3), index)

    return pl.pallas_call(
        functools.partial(_qkv_kernel, n_steps=n_steps),
        out_shape=[jax.ShapeDtypeStruct((batch, seq // dil, dil * gw3), BF16) for dil in dils],
        grid=(n_steps + 1,),
        in_specs=[
            pl.BlockSpec((tm, d), lambda s: (jnp.minimum((s + 1) // ng, n // tm - 1), 0)),
            pl.BlockSpec((1, d), lambda s: (0, 0)),
            pl.BlockSpec((ng, d, gw3), lambda s: (0, 0, 0), pipeline_mode=pl.Buffered(1)),
            pl.BlockSpec((1, LANES), lambda s: (0, 0)),
            pl.BlockSpec((1, LANES), lambda s: (0, 0)),
            tab, tab, tab,
        ],
        out_specs=[out_spec(gi, dil) for gi, dil in enumerate(dils)],
        scratch_shapes=[pltpu.VMEM((tm, d), BF16), pltpu.VMEM((tm, d), BF16),
                        pltpu.VMEM((tm, gw3), F32), pltpu.VMEM((tm, gw3), F32),
                        pltpu.VMEM((gw3 // LANES, tm, LANES), F32)],
        compiler_params=pltpu.CompilerParams(dimension_semantics=("arbitrary",),
                                             vmem_limit_bytes=QKV_VMEM_LIMIT),
        name="qkv_project",
    )(x2d, g, w3, qg, kg, cos_t, sa_t, sb_t)


def _attn_kernel(qkv_ref, o_ref, lse_ref, *, dilation, length, radius):
    gw = GROUP_WIDTH
    bq = ATTN_Q_BLOCK
    win = min(length, bq + 2 * radius)
    lane = lax.broadcasted_iota(jnp.int32, (bq, LANES), 1)
    tiles = [(r, qb) for r in range(dilation) for qb in range(length // bq)]
    for t0 in range(0, len(tiles), ATTN_TILES_PER_ROUND):
        work = []
        for r, qb in tiles[t0:t0 + ATTN_TILES_PER_ROUND]:
            q0 = qb * bq
            k0 = min(max(q0 - radius, 0), length - win)
            jq = q0 + lax.broadcasted_iota(jnp.int32, (bq, win), 0)
            jk = k0 + lax.broadcasted_iota(jnp.int32, (bq, win), 1)
            valid = jnp.abs(jk - jq) <= radius
            for hh in range(HEADS_PER_GROUP):
                c = r * 3 * gw + hh * HEAD_DIM
                q = qkv_ref[q0:q0 + bq, c:c + HEAD_DIM]
                k = qkv_ref[k0:k0 + win, c + gw:c + gw + HEAD_DIM]
                s = lax.dot_general(q, k, (((1,), (1,)), ((), ())), preferred_element_type=F32)
                work.append((r, q0, k0, hh, jnp.where(valid, s, NEG_INF)))
        soft = []
        for r, q0, k0, hh, s in work:
            m = jnp.max(s, axis=-1, keepdims=True)
            p = jnp.exp(s - m)
            l = jnp.sum(p, axis=-1, keepdims=True)
            soft.append((p.astype(BF16), l, m + jnp.log(l)))
        lse_tile = None
        for (r, q0, k0, hh, _), (p, l, lse) in zip(work, soft):
            c = r * 3 * gw + hh * HEAD_DIM
            v = qkv_ref[k0:k0 + win, c + 2 * gw:c + 2 * gw + HEAD_DIM]
            o = jnp.dot(p, v, preferred_element_type=F32) / l
            oc = r * gw + hh * HEAD_DIM
            o_ref[q0:q0 + bq, oc:oc + HEAD_DIM] = o.astype(BF16)
            lse_tile = jnp.where(lane == hh, lse, jnp.zeros((bq, LANES), F32) if hh == 0 else lse_tile)
            if hh == HEADS_PER_GROUP - 1:
                lse_ref[q0:q0 + bq, r * LANES:(r + 1) * LANES] = lse_tile


def _attention_group(qkv_g, gi, batch, seq):
    window, dilation = DILATED_PATTERNS[gi]
    radius = (window // 2) // dilation
    length = seq // dilation
    gw3 = 3 * GROUP_WIDTH
    kern = functools.partial(_attn_kernel, dilation=dilation, length=length, radius=radius)
    return pl.pallas_call(
        kern,
        out_shape=(jax.ShapeDtypeStruct((batch, length, dilation * GROUP_WIDTH), BF16),
                   jax.ShapeDtypeStruct((batch, length, dilation * LANES), F32)),
        grid=(batch,),
        in_specs=[pl.BlockSpec((None, length, dilation * gw3), lambda b: (b, 0, 0))],
        out_specs=(pl.BlockSpec((None, length, dilation * GROUP_WIDTH), lambda b: (b, 0, 0)),
                   pl.BlockSpec((None, length, dilation * LANES), lambda b: (b, 0, 0))),
        compiler_params=_params("parallel"),
        name=f"band_attention_d{dilation}",
    )(qkv_g)


def _attn_out_kernel(o0, o1, o2, l0, l1, l2, x_ref, w_ref, out_ref, mix_ref, o_rows, l_rows):
    tm = out_ref.shape[0]
    for gi, (o_ref, l_ref) in enumerate(((o0, l0), (o1, l1), (o2, l2))):
        dilation = DILATED_PATTERNS[gi][1]
        for phase in range(dilation):
            dst = pl.ds(phase, tm // dilation, stride=dilation)
            for hh in range(HEADS_PER_GROUP):
                col = phase * GROUP_WIDTH + hh * HEAD_DIM
                o_rows[gi * HEADS_PER_GROUP + hh, dst, :] = o_ref[:, col:col + HEAD_DIM].astype(F32)
            l_rows[gi, dst, :] = l_ref[:, phase * LANES:(phase + 1) * LANES]
    ls = [l_rows[gi] for gi in range(N_ATTN_GROUPS)]
    m = jnp.maximum(jnp.maximum(ls[0], ls[1]), ls[2])
    es = [jnp.exp(l - m) for l in ls]
    den = es[0] + es[1] + es[2]
    for gi in range(N_ATTN_GROUPS):
        alpha = es[gi] / den
        for hh in range(HEADS_PER_GROUP):
            c = hh * HEAD_DIM
            a = alpha[:, hh:hh + 1]
            mix_ref[:, gi * GROUP_WIDTH + c:gi * GROUP_WIDTH + c + HEAD_DIM] = (
                o_rows[gi * HEADS_PER_GROUP + hh] * a).astype(BF16)
    out_ref[...] = x_ref[...] + jnp.dot(mix_ref[...], w_ref[...], preferred_element_type=F32)


def _attn_out_project(os_, lses, x2d, w_out, seq):
    n, d = x2d.shape
    tm = ROW_TILE
    seq_tiles = seq // tm
    width = N_ATTN_GROUPS * GROUP_WIDTH
    dils = [dil for _, dil in DILATED_PATTERNS]
    pm = lambda i: (i // seq_tiles, i % seq_tiles, 0)
    return pl.pallas_call(
        _attn_out_kernel,
        out_shape=jax.ShapeDtypeStruct((n, d), F32),
        grid=(n // tm,),
        in_specs=[pl.BlockSpec((None, tm // dil, dil * GROUP_WIDTH), pm) for dil in dils]
                 + [pl.BlockSpec((None, tm // dil, dil * LANES), pm) for dil in dils]
                 + [pl.BlockSpec((tm, d), lambda i: (i, 0)),
                    pl.BlockSpec((width, d), lambda i: (0, 0))],
        out_specs=pl.BlockSpec((tm, d), lambda i: (i, 0)),
        scratch_shapes=[pltpu.VMEM((tm, width), BF16),
                        pltpu.VMEM((N_ATTN_GROUPS * HEADS_PER_GROUP, tm, HEAD_DIM), F32),
                        pltpu.VMEM((N_ATTN_GROUPS, tm, LANES), F32)],
        compiler_params=_params("parallel"),
        name="attn_out_project",
    )(*os_, *lses, x2d, w_out)


U32 = jnp.uint32
HIGH_HALF = np.uint32(0xFFFF0000)


def _pack_halves(x):
    c = x.shape[1] // 2
    lo = lax.bitcast_convert_type(x[:, :c].astype(BF16).astype(F32), U32)
    hi = lax.bitcast_convert_type(x[:, c:].astype(BF16).astype(F32), U32)
    return (lo >> 16) | (hi & HIGH_HALF)


def _unpack_halves(u):
    lo = lax.bitcast_convert_type(u << 16, F32)
    hi = lax.bitcast_convert_type(u & HIGH_HALF, F32)
    return jnp.concatenate([lo, hi], axis=1)


def _router_kernel(x_ref, g_ref, w_ref, b_ref, sel_ref, hp_ref, cnt_ref):
    hf = _rms(x_ref[...], g_ref[...])
    hp_ref[...] = _pack_halves(hf)
    h = hf.astype(BF16)
    logits = jnp.dot(h, w_ref[...], preferred_element_type=F32) + b_ref[...]
    lane = lax.broadcasted_iota(jnp.int32, logits.shape, 1)
    lanef = lane.astype(F32)
    big = float(LANES)
    is_grp = lane < N_EXPERT_GROUPS
    coarse = jnp.where(is_grp, logits, -jnp.inf)
    cmax = jnp.max(coarse, axis=-1, keepdims=True)
    g_sel = jnp.min(jnp.where(coarse == cmax, lanef, big), axis=-1, keepdims=True)
    den = jnp.sum(jnp.where(is_grp, jnp.exp(logits - cmax), 0.0), axis=-1, keepdims=True)
    g_gate = 1.0 / den
    lo = N_EXPERT_GROUPS + g_sel * EXPERTS_PER_GROUP
    in_grp = (lanef >= lo) & (lanef < lo + EXPERTS_PER_GROUP)
    fine = jnp.where(in_grp, logits, -jnp.inf)
    v1 = jnp.max(fine, axis=-1, keepdims=True)
    i1 = jnp.min(jnp.where(fine == v1, lanef, big), axis=-1, keepdims=True)
    fine2 = jnp.where(lanef == i1, -jnp.inf, fine)
    v2 = jnp.max(fine2, axis=-1, keepdims=True)
    i2 = jnp.min(jnp.where(fine2 == v2, lanef, big), axis=-1, keepdims=True)
    e2 = jnp.exp(v2 - v1)
    w1 = g_gate * (1.0 / (1.0 + e2))
    w2 = g_gate * (e2 / (1.0 + e2))
    sel = jnp.where(lane == 0, w1, 0.0)
    sel = jnp.where(lane == 1, w2, sel)
    sel = jnp.where(lane == 2, i1 - N_EXPERT_GROUPS, sel)
    sel = jnp.where(lane == 3, i2 - N_EXPERT_GROUPS, sel)
    sel_ref[...] = sel
    chosen = (lanef == i1) | (lanef == i2)
    cnt_ref[...] = jnp.sum(jnp.where(chosen, 1.0, 0.0), axis=0, keepdims=True)


def _route(x2d, g, w_r, b_r):
    n, d = x2d.shape
    tm = ROW_TILE
    return pl.pallas_call(
        _router_kernel,
        out_shape=(jax.ShapeDtypeStruct((n, LANES), F32), jax.ShapeDtypeStruct((n, d // 2), U32),
                   jax.ShapeDtypeStruct((n // tm, 1, LANES), F32)),
        grid=(n // tm,),
        in_specs=[pl.BlockSpec((tm, d), lambda i: (i, 0)),
                  pl.BlockSpec((1, d), lambda i: (0, 0)),
                  pl.BlockSpec((d, LANES), lambda i: (0, 0)),
                  pl.BlockSpec((1, LANES), lambda i: (0, 0))],
        out_specs=(pl.BlockSpec((tm, LANES), lambda i: (i, 0)), pl.BlockSpec((tm, d // 2), lambda i: (i, 0)),
                   pl.BlockSpec((None, 1, LANES), lambda i: (i, 0, 0))),
        compiler_params=_params("parallel"),
        name="moe_router",
    )(x2d, g, w_r, b_r)


def _expert_kernel(be_ref, first_ref, ws_ref, ne_ref, nu_ref,
                   src_ref, dst_ref,
                   h_hbm, wg_hbm, wu_hbm, wd_hbm,
                   buf_hbm,
                   xbuf, ybuf, wgb, wub, wdb, gsem, ssem, wsem, *, layer):
    i = pl.program_id(0)
    n_used = nu_ref[0]
    slot = lax.rem(i, SCATTER_SLOTS)
    prev_slot = lax.rem(i + SCATTER_SLOTS - 1, SCATTER_SLOTS)
    gslot = lax.rem(i, GATHER_SLOTS)
    rows = MOE_BLOCK

    last_block = src_ref.shape[0] - 1

    def start_gather(block, s):
        block = jnp.minimum(block, last_block)
        for r in range(rows):
            pltpu.make_async_copy(h_hbm.at[pl.ds(src_ref[block, r], 1)], xbuf.at[s, pl.ds(r, 1)],
                                  gsem.at[s]).start()

    def wait_gather(s):
        pltpu.make_async_copy(h_hbm.at[pl.ds(0, rows)], xbuf.at[s], gsem.at[s]).wait()

    def start_scatter(s):
        for r in range(rows):
            pltpu.make_async_copy(ybuf.at[s, pl.ds(r, 1)], buf_hbm.at[pl.ds(dst_ref[i, r], 1)],
                                  ssem.at[s]).start()

    def wait_scatter(s):
        pltpu.make_async_copy(ybuf.at[s], buf_hbm.at[pl.ds(0, rows)], ssem.at[s]).wait()

    def weight_copies(e, s):
        return [(pltpu.make_async_copy(hbm.at[layer, e], vmem.at[s], wsem.at[s]), 1)
                for hbm, vmem in ((wg_hbm, wgb), (wu_hbm, wub), (wd_hbm, wdb))]

    @pl.when(i == 0)
    def _():
        ybuf[...] = jnp.zeros_like(ybuf)
        for b in range(GATHER_AHEAD):
            start_gather(b, b)
        for c, q in weight_copies(be_ref[0], 0):
            c.start(priority=q)

    @pl.when(i < n_used)
    def _():
        wait_gather(gslot)

        @pl.when(i >= SCATTER_SLOTS - 1)
        def _():
            wait_scatter(slot)

        @pl.when(first_ref[i] == 1)
        def _():
            s = ws_ref[i]
            for c, _ in weight_copies(be_ref[i], s):
                c.wait()

            @pl.when(ne_ref[i] >= 0)
            def _():
                for c, q in weight_copies(ne_ref[i], 1 - s):
                    c.start(priority=q)

        start_scatter(prev_slot)
        ws = ws_ref[i]
        h = _unpack_halves(xbuf[gslot]).astype(BF16)
        gate = jnp.dot(h, wgb[ws].astype(BF16), preferred_element_type=F32)
        up = jnp.dot(h, wub[ws].astype(BF16), preferred_element_type=F32)
        act = (gate * (1.0 / (1.0 + jnp.exp(-gate))) * up).astype(BF16)
        ybuf[slot] = _pack_halves(jnp.dot(act, wdb[ws].astype(BF16), preferred_element_type=F32))
        start_gather(i + GATHER_AHEAD, lax.rem(i + GATHER_AHEAD, GATHER_SLOTS))

    @pl.when(i == n_used)
    def _():
        start_scatter(prev_slot)
        for b in range(SCATTER_SLOTS):
            @pl.when(i >= b)
            def _(b=b):
                wait_scatter(lax.rem(i + 2 * SCATTER_SLOTS - 1 - b, SCATTER_SLOTS))
        for b in range(GATHER_AHEAD):
            wait_gather(lax.rem(i + b, GATHER_SLOTS))


def _expert_mlp(hp, w_gate, w_up, w_down, layer, tables, slot_src, slot_dst):
    n, dp = hp.shape
    d_exp = w_gate.shape[-1]
    d = w_gate.shape[-2]
    rows = MOE_BLOCK
    n_blocks = slot_src.shape[0]
    grid_spec = pltpu.PrefetchScalarGridSpec(
        num_scalar_prefetch=len(tables) + 2,
        grid=(n_blocks + 1,),
        in_specs=[
            pl.BlockSpec(memory_space=pl.ANY),
            pl.BlockSpec(memory_space=pl.ANY),
            pl.BlockSpec(memory_space=pl.ANY),
            pl.BlockSpec(memory_space=pl.ANY),
        ],
        out_specs=pl.BlockSpec(memory_space=pl.ANY),
        scratch_shapes=[pltpu.VMEM((GATHER_SLOTS, rows, dp), U32), pltpu.VMEM((SCATTER_SLOTS, rows, dp), U32),
                        pltpu.VMEM((2, d, d_exp), F32), pltpu.VMEM((2, d, d_exp), F32),
                        pltpu.VMEM((2, d_exp, d), F32),
                        pltpu.SemaphoreType.DMA((GATHER_SLOTS,)), pltpu.SemaphoreType.DMA((SCATTER_SLOTS,)),
                        pltpu.SemaphoreType.DMA((2,))],
    )
    return pl.pallas_call(
        functools.partial(_expert_kernel, layer=layer),
        out_shape=jax.ShapeDtypeStruct((TOP_K * n + rows, dp), U32),
        grid_spec=grid_spec,
        compiler_params=_params("arbitrary"),
        name="moe_expert_mlp",
    )(*tables, slot_src, slot_dst, hp, w_gate, w_up, w_down)


def _dispatch_plan(sel, tile_counts, n):
    n_assign = n * TOP_K
    n_blocks = -(-n_assign // MOE_BLOCK) + N_EXPERTS
    e_flat = sel[:, TOP_K:2 * TOP_K].astype(jnp.int32).reshape(-1)
    experts = jnp.arange(N_EXPERTS, dtype=jnp.int32)
    lo = N_EXPERT_GROUPS
    counts = jnp.sum(tile_counts[:, 0, lo:lo + N_EXPERTS], axis=0).astype(jnp.int32)
    padded = ((counts + MOE_BLOCK - 1) // MOE_BLOCK) * MOE_BLOCK
    pend = jnp.cumsum(padded)
    pstart = pend - padded
    start = jnp.cumsum(counts) - counts
    order = jnp.argsort(e_flat, stable=True).astype(jnp.int32)
    n_used = pend[-1] // MOE_BLOCK
    blk = jnp.arange(n_blocks, dtype=jnp.int32)
    used = blk < n_used

    def per_block(onehot, table):
        return jnp.sum(jnp.where(onehot, table[None, :], 0), axis=1)

    block_e = jnp.minimum(jnp.sum((pend[None, :] <= (blk * MOE_BLOCK)[:, None]).astype(jnp.int32), axis=1),
                          N_EXPERTS - 1)
    last_e = jnp.sum(jnp.where(blk == n_used - 1, block_e, 0))
    block_e = jnp.where(used, block_e, last_e)
    onehot = block_e[:, None] == experts[None, :]
    first_row = blk * MOE_BLOCK - per_block(onehot, pstart)
    n_valid = jnp.where(used, jnp.clip(per_block(onehot, counts) - first_row, 0, MOE_BLOCK), 0)
    nonempty = counts > 0
    w_slot = lax.rem(jnp.cumsum(nonempty.astype(jnp.int32)) - 1, 2)
    later = nonempty[None, :] & (experts[None, :] > experts[:, None])
    next_e = jnp.min(jnp.where(later, experts[None, :], N_EXPERTS), axis=1)
    next_e = jnp.where(next_e == N_EXPERTS, -1, next_e)
    i32 = lambda v: v.astype(jnp.int32)
    pad1 = lambda v, fill: i32(jnp.concatenate([v, jnp.full((1,), fill, v.dtype)]))
    tables = (pad1(block_e, 0), pad1(i32(used & (first_row == 0)), 0),
              pad1(per_block(onehot, w_slot), 0), pad1(per_block(onehot, next_e), -1),
              i32(n_used.reshape(1)))
    in_blk = jnp.arange(MOE_BLOCK, dtype=jnp.int32)[None, :]
    sorted_pos = (per_block(onehot, start) + first_row)[:, None] + in_blk
    a = order[jnp.clip(sorted_pos, 0, n_assign - 1)]
    valid = in_blk < n_valid[:, None]
    tok = a // TOP_K
    tail = jnp.broadcast_to(TOP_K * n + in_blk, a.shape)
    slot_src = i32(jnp.where(valid, tok, 0))
    slot_dst = jnp.where(valid, (a % TOP_K) * n + tok, tail)
    slot_dst = i32(jnp.concatenate([tail[:1], slot_dst]))
    return tables, slot_src, slot_dst


def _combined(x_ref, sel_ref, y0_ref, y1_ref):
    sel = sel_ref[...]
    y0, y1 = _unpack_halves(y0_ref[...]), _unpack_halves(y1_ref[...])
    return x_ref[...] + (y0 * sel[:, 0:1] + y1 * sel[:, 1:2])


def _pending_specs(tm, n, d):
    tiles = n // tm
    return [pl.BlockSpec((tm, d), lambda i: (i, 0)),
            pl.BlockSpec((tm, LANES), lambda i: (i, 0)),
            pl.BlockSpec((tm, d // 2), lambda i: (i, 0)),
            pl.BlockSpec((tm, d // 2), lambda i: (i + tiles, 0))]


def _combine_kernel(x_ref, sel_ref, y0_ref, y1_ref, o_ref):
    o_ref[...] = _combined(x_ref, sel_ref, y0_ref, y1_ref)


def _combine(pending):
    x2d, sel, buf = pending
    n, d = x2d.shape
    tm = ROW_TILE
    return pl.pallas_call(
        _combine_kernel,
        out_shape=jax.ShapeDtypeStruct((n, d), F32),
        grid=(n // tm,),
        in_specs=_pending_specs(tm, n, d),
        out_specs=pl.BlockSpec((tm, d), lambda i: (i, 0)),
        compiler_params=_params("parallel"),
        name="moe_combine",
    )(x2d, sel, buf, buf)


def _moe(x2d, g, w_rg, b_rg, w_re, b_re, w_gate, w_up, w_down, layer):
    n, d = x2d.shape
    pad = LANES - N_EXPERT_GROUPS - N_EXPERTS
    w_r = jnp.concatenate([w_rg, w_re, jnp.zeros((d, pad), F32)], axis=1).astype(BF16)
    b_r = jnp.concatenate([b_rg, b_re, jnp.zeros((pad,), F32)]).reshape(1, LANES)
    sel, hp, tile_counts = _route(x2d, g.reshape(1, d), w_r, b_r)
    tables, slot_src, slot_dst = _dispatch_plan(sel, tile_counts, n)
    buf = _expert_mlp(hp, w_gate, w_up, w_down, layer, tables, slot_src, slot_dst)
    return x2d, sel, buf


def _fourier_in_kernel(x_ref, sel_ref, y0_ref, y1_ref, g_ref, w_ref, ccs_ref, nyw_ref,
                       x_out_ref, a_ref, b_ref, ny_ref):
    x = _combined(x_ref, sel_ref, y0_ref, y1_ref)
    x_out_ref[...] = x
    h = _rms(x, g_ref[...]).astype(BF16)
    u = jnp.dot(h, w_ref[...], preferred_element_type=F32).astype(BF16)
    gd = ccs_ref.shape[0]
    half = gd // 2
    for gi in range(FOURIER_GROUPS):
        ab = jnp.dot(u[:, gi * gd:(gi + 1) * gd], ccs_ref[...], preferred_element_type=F32).astype(BF16)
        a_ref[:, gi * half:(gi + 1) * half] = ab[:, :half]
        b_ref[:, gi * half:(gi + 1) * half] = ab[:, half:]
    ny_ref[...] = jnp.dot(u, nyw_ref[...], preferred_element_type=F32).astype(BF16)


def _fourier_in(pending, g, w_in, ccs, nyw):
    x2d, sel, buf = pending
    n, d = x2d.shape
    tm = ROW_TILE // 2
    gd = d // FOURIER_GROUPS
    half = jax.ShapeDtypeStruct((n, d // 2), BF16)
    return pl.pallas_call(
        _fourier_in_kernel,
        out_shape=(jax.ShapeDtypeStruct((n, d), F32), half, half, jax.ShapeDtypeStruct((n, LANES), BF16)),
        grid=(n // tm,),
        in_specs=_pending_specs(tm, n, d)
                 + [pl.BlockSpec((1, d), lambda i: (0, 0)),
                    pl.BlockSpec((d, d), lambda i: (0, 0)),
                    pl.BlockSpec((gd, gd), lambda i: (0, 0)),
                    pl.BlockSpec((d, LANES), lambda i: (0, 0))],
        out_specs=(pl.BlockSpec((tm, d), lambda i: (i, 0)),
                   pl.BlockSpec((tm, d // 2), lambda i: (i, 0)),
                   pl.BlockSpec((tm, d // 2), lambda i: (i, 0)),
                   pl.BlockSpec((tm, LANES), lambda i: (i, 0))),
        compiler_params=_params("parallel"),
        name="fourier_in",
    )(x2d, sel, buf, buf, g, w_in, ccs, nyw)


def _seq_dft_kernel(cs_ref, nss_ref, a_ref, b_ref, ny_ref, fd_ref, fs_ref, fn_ref):
    p = jnp.dot(cs_ref[...], a_ref[...], preferred_element_type=F32)
    qn = jnp.dot(nss_ref[...], b_ref[...], preferred_element_type=F32)
    fd_ref[...] = (p + qn).astype(BF16)
    fs_ref[...] = (p - qn).astype(BF16)

    @pl.when(pl.program_id(2) == 0)
    def _():
        fn_ref[...] = jnp.dot(cs_ref[...], ny_ref[...], preferred_element_type=F32).astype(BF16)


def _seq_dft(cs, nss, a3, b3, ny3):
    batch, seq, dh = a3.shape
    tm = min(seq, 1024)
    tn = min(dh, 512)
    half = jax.ShapeDtypeStruct((batch, seq, dh), BF16)
    col = pl.BlockSpec((None, seq, tn), lambda b, i, j: (b, 0, j))
    out = pl.BlockSpec((None, tm, tn), lambda b, i, j: (b, i, j))
    return pl.pallas_call(
        _seq_dft_kernel,
        out_shape=(half, half, jax.ShapeDtypeStruct((batch, seq, LANES), BF16)),
        grid=(batch, seq // tm, dh // tn),
        in_specs=[pl.BlockSpec((tm, seq), lambda b, i, j: (i, 0)),
                  pl.BlockSpec((tm, seq), lambda b, i, j: (i, 0)),
                  col, col,
                  pl.BlockSpec((None, seq, LANES), lambda b, i, j: (b, 0, 0))],
        out_specs=(out, out, pl.BlockSpec((None, tm, LANES), lambda b, i, j: (b, i, 0))),
        compiler_params=_params("parallel", "parallel", "arbitrary"),
        name="seq_dft",
    )(cs, nss, a3, b3, ny3)


def _proj_residual_kernel(*refs):
    m = (len(refs) - 2) // 2
    x_ref, o_ref = refs[2 * m], refs[2 * m + 1]
    acc = x_ref[...]
    for f_ref, w_ref in zip(refs[:m], refs[m:2 * m]):
        acc = acc + jnp.dot(f_ref[...], w_ref[...], preferred_element_type=F32)
    o_ref[...] = acc


def _proj_residual(fs, ws, x2d):
    n, d = x2d.shape
    tm = ROW_TILE
    return pl.pallas_call(
        _proj_residual_kernel,
        out_shape=jax.ShapeDtypeStruct((n, d), F32),
        grid=(n // tm,),
        in_specs=[pl.BlockSpec((tm, f.shape[1]), lambda i: (i, 0)) for f in fs]
                 + [pl.BlockSpec(w.shape, lambda i: (0, 0)) for w in ws]
                 + [pl.BlockSpec((tm, d), lambda i: (i, 0))],
        out_specs=pl.BlockSpec((tm, d), lambda i: (i, 0)),
        compiler_params=_params("parallel"),
        name="proj_residual",
    )(*fs, *ws, x2d)


def _rope_tables(seq):
    half = ROT_DIM // 2
    inv_freq = ROPE_THETA ** (-np.arange(0, ROT_DIM, 2, dtype=np.float64) / ROT_DIM)
    ang = np.arange(seq, dtype=np.float64)[:, None] * inv_freq[None, :]
    cos_t = np.ones((seq, LANES))
    sa = np.zeros((seq, LANES))
    sb = np.zeros((seq, LANES))
    cos_t[:, :half] = np.cos(ang)
    cos_t[:, half:ROT_DIM] = np.cos(ang)
    sa[:, half:ROT_DIM] = np.sin(ang)
    sb[:, :half] = -np.sin(ang)
    return [jnp.asarray(t, F32) for t in (cos_t, sa, sb)]


def _dft_tables(n):
    jk = (np.arange(n, dtype=np.int64)[:, None] * np.arange(n, dtype=np.int64)[None, :]) % n
    ang = 2.0 * np.pi * jk.astype(np.float64) / n
    scale = n ** -0.5
    return np.cos(ang) * scale, np.sin(ang) * scale


def _attention_mixer(x2d, batch, seq, norm_g, w_qkv, q_g, k_g, w_out):
    n, d = x2d.shape
    width = N_ATTN_GROUPS * GROUP_WIDTH
    gw = GROUP_WIDTH
    w3 = jnp.stack([jnp.concatenate([w_qkv[:, t * width + gi * gw:t * width + (gi + 1) * gw]
                                     for t in range(3)], axis=1)
                    for gi in range(N_ATTN_GROUPS)]).astype(BF16)
    qg = (q_g * HEAD_DIM ** -0.5).reshape(1, HEAD_DIM)
    kg = k_g.reshape(1, HEAD_DIM)
    cos_t, sa_t, sb_t = _rope_tables(seq)
    qkv = _qkv_project(x2d, norm_g.reshape(1, d), w3, qg, kg, cos_t, sa_t, sb_t, batch, seq)
    outs = [_attention_group(qkv[gi], gi, batch, seq) for gi in range(N_ATTN_GROUPS)]
    return _attn_out_project([o for o, _ in outs], [l for _, l in outs], x2d, w_out.astype(BF16), seq)


def _fourier_mixer(pending, batch, seq, norm_g, w_in, w_out):
    n, d = pending[0].shape
    groups = FOURIER_GROUPS
    gd = d // groups
    half = gd // 2
    cc, sc = _dft_tables(gd)
    cs, ss = _dft_tables(seq)
    ccs = np.concatenate([cc[:, :half], sc[:, :half]], axis=1)
    nyw = np.zeros((d, LANES))
    for gi in range(groups):
        nyw[gi * gd:(gi + 1) * gd, gi] = cc[:, half]
    to_bf16 = lambda t: jnp.asarray(t, F32).astype(BF16)
    x2d, a, b, ny = _fourier_in(pending, norm_g.reshape(1, d), w_in.astype(BF16), to_bf16(ccs), to_bf16(nyw))
    fd, fs, fn = _seq_dft(to_bf16(cs), to_bf16(-ss), a.reshape(batch, seq, d // 2), b.reshape(batch, seq, d // 2),
                          ny.reshape(batch, seq, LANES))
    k = np.arange(half)
    base = (np.arange(groups) * gd)[:, None]
    rows_s = (base + (gd - k[None, :]) % gd).reshape(-1)
    keep_s = np.tile(k > 0, groups)
    w3 = w_out.reshape(groups, gd, d)
    w_d = w3[:, :half].reshape(groups * half, d).astype(BF16)
    w_s = jnp.where(jnp.asarray(keep_s)[:, None], w_out[rows_s], 0.0).astype(BF16)
    w_n = jnp.concatenate([w3[:, half], jnp.zeros((LANES - groups, d), w_out.dtype)]).astype(BF16)
    flat = lambda t: t.reshape(n, t.shape[-1])
    return _proj_residual([flat(fd), flat(fs), flat(fn)], [w_d, w_s, w_n], x2d)


def kernel(x, attn_norm_g, w_qkv, q_norm_g, k_norm_g, w_attn_out, fourier_norm_g, w_fourier_in, w_fourier_out, moe_norm_g, w_router_group, b_router_group, w_router_expert, b_router_expert, w_expert_gate, w_expert_up, w_expert_down):
    batch, seq, d = x.shape
    depth = moe_norm_g.shape[0]
    h = x.reshape(batch * seq, d)
    pending = None
    for i in range(depth):
        j = i // 2
        if i % 2 == 0:
            if pending is not None:
                h = _combine(pending)
            h = _attention_mixer(h, batch, seq, attn_norm_g[j], w_qkv[j], q_norm_g[j], k_norm_g[j], w_attn_out[j])
        else:
            h = _fourier_mixer(pending, batch, seq, fourier_norm_g[j], w_fourier_in[j], w_fourier_out[j])
        pending = _moe(h, moe_norm_g[i], w_router_group[i], b_router_group[i], w_router_expert[i],
                       b_router_expert[i], w_expert_gate, w_expert_up, w_expert_down, i)
    return _combine(pending).reshape(batch, seq, d)
```

```python
import functools

import numpy as np
import jax
import jax.numpy as jnp
from jax import lax
from jax.experimental import pallas as pl
from jax.experimental.pallas import tpu as pltpu

F32 = jnp.float32
BF16 = jnp.bfloat16

HEAD_DIM = 128
HEADS_PER_GROUP = 4
DILATED_PATTERNS = ((128, 1), (512, 4), (2048, 16))
N_ATTN_GROUPS = len(DILATED_PATTERNS)
GROUP_WIDTH = HEADS_PER_GROUP * HEAD_DIM
ROT_DIM = HEAD_DIM // 4
ROPE_THETA = 500000.0
NEG_INF = -1e30
FOURIER_GROUPS = 8
N_EXPERT_GROUPS = 8
EXPERTS_PER_GROUP = 8
N_EXPERTS = N_EXPERT_GROUPS * EXPERTS_PER_GROUP
TOP_K = 2
MOE_BLOCK = 128
GATHER_AHEAD = 8
GATHER_SLOTS = GATHER_AHEAD + 1
SCATTER_SLOTS = 4
EPS = 1e-6

LANES = 128
ATTN_Q_BLOCK = 128
ATTN_TILES_PER_ROUND = 2
ROW_TILE = 512
VMEM_LIMIT = 48 * 1024 * 1024
QKV_VMEM_LIMIT = 56 * 1024 * 1024


def _params(*sem):
    return pltpu.CompilerParams(dimension_semantics=sem, vmem_limit_bytes=VMEM_LIMIT)


def _rms(x, g):
    ms = jnp.mean(x * x, axis=-1, keepdims=True)
    return x * lax.rsqrt(ms + EPS) * g


def _qkv_kernel(x_ref, g_ref, w_ref, qg_ref, kg_ref, cos_ref, sa_ref, sb_ref, *rest, n_steps):
    o_refs, (h_ref, h_next, r_new, r_old, c_ref) = rest[:N_ATTN_GROUPS], rest[N_ATTN_GROUPS:]
    s = pl.program_id(0)
    chunks, tm, _ = c_ref.shape

    def project():
        w = w_ref[lax.rem(s, N_ATTN_GROUPS)]
        r_new[...] = jnp.dot(h_ref[...], w, preferred_element_type=F32)

    def finish(gi):
        cos, sa, sb = cos_ref[...], sa_ref[...], sb_ref[...]
        for j in range(2 * HEADS_PER_GROUP):
            t = r_old[:, j * HEAD_DIM:(j + 1) * HEAD_DIM]
            t = _rms(t, qg_ref[...] if j < HEADS_PER_GROUP else kg_ref[...])
            t = t * cos + pltpu.roll(t, ROT_DIM // 2, 1) * sa + pltpu.roll(t, HEAD_DIM - ROT_DIM // 2, 1) * sb
            c_ref[j] = t
        for j in range(2 * HEADS_PER_GROUP, 3 * HEADS_PER_GROUP):
            c_ref[j] = r_old[:, j * HEAD_DIM:(j + 1) * HEAD_DIM]
        dilation = DILATED_PATTERNS[gi][1]
        for phase in range(dilation):
            for c in range(chunks):
                rows = c_ref[c, pl.ds(phase, tm // dilation, stride=dilation), :]
                col = (phase * chunks + c) * LANES
                o_refs[gi][:, col:col + LANES] = rows.astype(BF16)

    def hand_over():
        r_old[...] = r_new[...]

    @pl.when(s == 0)
    def _():
        h_ref[...] = _rms(x_ref[...], g_ref[...]).astype(BF16)
        project()
        hand_over()

    for gi in range(N_ATTN_GROUPS):
        prev_is_gi = lax.rem(s + N_ATTN_GROUPS - 1, N_ATTN_GROUPS) == gi

        @pl.when((s >= 1) & (s < n_steps) & prev_is_gi)
        def _(gi=gi):
            project()
            finish(gi)
            hand_over()
            if gi == N_ATTN_GROUPS - 2:
                h_next[...] = _rms(x_ref[...], g_ref[...]).astype(BF16)
                h_ref[...] = h_next[...]

    pl.when(s == n_steps)(functools.partial(finish, (n_steps - 1) % N_ATTN_GROUPS))


def _qkv_project(x2d, g, w3, qg, kg, cos_t, sa_t, sb_t, batch, seq):
    n, d = x2d.shape
    tm = ROW_TILE
    seq_tiles = seq // tm
    gw3 = 3 * GROUP_WIDTH
    ng = N_ATTN_GROUPS
    n_steps = (n // tm) * ng
    prev_tile = lambda s: jnp.maximum(s - 1, 0) // ng
    tab = pl.BlockSpec((tm, LANES), lambda s: (prev_tile(s) % seq_tiles, 0))
    dils = [dil for _, dil in DILATED_PATTERNS]

    def out_spec(gi, dil):
        def index(s):
            t = jnp.maximum(s - 1 - gi, 0) // ng
            return (t // seq_tiles, t % seq_tiles, 0)
        return pl.BlockSpec((None, tm // dil, dil * gw3), index)

    return pl.pallas_call(
        functools.partial(_qkv_kernel, n_steps=n_steps),
        out_shape=[jax.ShapeDtypeStruct((batch, seq // dil, dil * gw3), BF16) for dil in dils],
        grid=(n_steps + 1,),
        in_specs=[
            pl.BlockSpec((tm, d), lambda s: (jnp.minimum((s + 1) // ng, n // tm - 1), 0)),
            pl.BlockSpec((1, d), lambda s: (0, 0)),
            pl.BlockSpec((ng, d, gw3), lambda s: (0, 0, 0), pipeline_mode=pl.Buffered(1)),
            pl.BlockSpec((1, LANES), lambda s: (0, 0)),
            pl.BlockSpec((1, LANES), lambda s: (0, 0)),
            tab, tab, tab,
        ],
        out_specs=[out_spec(gi, dil) for gi, dil in enumerate(dils)],
        scratch_shapes=[pltpu.VMEM((tm, d), BF16), pltpu.VMEM((tm, d), BF16),
                        pltpu.VMEM((tm, gw3), F32), pltpu.VMEM((tm, gw3), F32),
                        pltpu.VMEM((gw3 // LANES, tm, LANES), F32)],
        compiler_params=pltpu.CompilerParams(dimension_semantics=("arbitrary",),
                                             vmem_limit_bytes=QKV_VMEM_LIMIT),
        name="qkv_project",
    )(x2d, g, w3, qg, kg, cos_t, sa_t, sb_t)


def _attn_kernel(qkv_ref, o_ref, lse_ref, *, dilation, length, radius):
    gw = GROUP_WIDTH
    bq = ATTN_Q_BLOCK
    win = min(length, bq + 2 * radius)
    lane = lax.broadcasted_iota(jnp.int32, (bq, LANES), 1)
    tiles = [(r, qb) for r in range(dilation) for qb in range(length // bq)]
    for t0 in range(0, len(tiles), ATTN_TILES_PER_ROUND):
        work = []
        for r, qb in tiles[t0:t0 + ATTN_TILES_PER_ROUND]:
            q0 = qb * bq
            k0 = min(max(q0 - radius, 0), length - win)
            jq = q0 + lax.broadcasted_iota(jnp.int32, (bq, win), 0)
            jk = k0 + lax.broadcasted_iota(jnp.int32, (bq, win), 1)
            valid = jnp.abs(jk - jq) <= radius
            for hh in range(HEADS_PER_GROUP):
                c = r * 3 * gw + hh * HEAD_DIM
                q = qkv_ref[q0:q0 + bq, c:c + HEAD_DIM]
                k = qkv_ref[k0:k0 + win, c + gw:c + gw + HEAD_DIM]
                s = lax.dot_general(q, k, (((1,), (1,)), ((), ())), preferred_element_type=F32)
                work.append((r, q0, k0, hh, jnp.where(valid, s, NEG_INF)))
        soft = []
        for r, q0, k0, hh, s in work:
            m = jnp.max(s, axis=-1, keepdims=True)
            p = jnp.exp(s - m)
            l = jnp.sum(p, axis=-1, keepdims=True)
            soft.append((p.astype(BF16), l, m + jnp.log(l)))
        lse_tile = None
        for (r, q0, k0, hh, _), (p, l, lse) in zip(work, soft):
            c = r * 3 * gw + hh * HEAD_DIM
            v = qkv_ref[k0:k0 + win, c + 2 * gw:c + 2 * gw + HEAD_DIM]
            o = jnp.dot(p, v, preferred_element_type=F32) / l
            oc = r * gw + hh * HEAD_DIM
            o_ref[q0:q0 + bq, oc:oc + HEAD_DIM] = o.astype(BF16)
            lse_tile = jnp.where(lane == hh, lse, jnp.zeros((bq, LANES), F32) if hh == 0 else lse_tile)
            if hh == HEADS_PER_GROUP - 1:
                lse_ref[q0:q0 + bq, r * LANES:(r + 1) * LANES] = lse_tile


def _attention_group(qkv_g, gi, batch, seq):
    window, dilation = DILATED_PATTERNS[gi]
    radius = (window // 2) // dilation
    length = seq // dilation
    gw3 = 3 * GROUP_WIDTH
    kern = functools.partial(_attn_kernel, dilation=dilation, length=length, radius=radius)
    return pl.pallas_call(
        kern,
        out_shape=(jax.ShapeDtypeStruct((batch, length, dilation * GROUP_WIDTH), BF16),
                   jax.ShapeDtypeStruct((batch, length, dilation * LANES), F32)),
        grid=(batch,),
        in_specs=[pl.BlockSpec((None, length, dilation * gw3), lambda b: (b, 0, 0))],
        out_specs=(pl.BlockSpec((None, length, dilation * GROUP_WIDTH), lambda b: (b, 0, 0)),
                   pl.BlockSpec((None, length, dilation * LANES), lambda b: (b, 0, 0))),
        compiler_params=_params("parallel"),
        name=f"band_attention_d{dilation}",
    )(qkv_g)


def _attn_out_kernel(o0, o1, o2, l0, l1, l2, x_ref, w_ref, out_ref, mix_ref, o_rows, l_rows):
    tm = out_ref.shape[0]
    for gi, (o_ref, l_ref) in enumerate(((o0, l0), (o1, l1), (o2, l2))):
        dilation = DILATED_PATTERNS[gi][1]
        for phase in range(dilation):
            dst = pl.ds(phase, tm // dilation, stride=dilation)
            for hh in range(HEADS_PER_GROUP):
                col = phase * GROUP_WIDTH + hh * HEAD_DIM
                o_rows[gi * HEADS_PER_GROUP + hh, dst, :] = o_ref[:, col:col + HEAD_DIM].astype(F32)
            l_rows[gi, dst, :] = l_ref[:, phase * LANES:(phase + 1) * LANES]
    ls = [l_rows[gi] for gi in range(N_ATTN_GROUPS)]
    m = jnp.maximum(jnp.maximum(ls[0], ls[1]), ls[2])
    es = [jnp.exp(l - m) for l in ls]
    den = es[0] + es[1] + es[2]
    for gi in range(N_ATTN_GROUPS):
        alpha = es[gi] / den
        for hh in range(HEADS_PER_GROUP):
            c = hh * HEAD_DIM
            a = alpha[:, hh:hh + 1]
            mix_ref[:, gi * GROUP_WIDTH + c:gi * GROUP_WIDTH + c + HEAD_DIM] = (
                o_rows[gi * HEADS_PER_GROUP + hh] * a).astype(BF16)
    out_ref[...] = x_ref[...] + jnp.dot(mix_ref[...], w_ref[...], preferred_element_type=F32)


def _attn_out_project(os_, lses, x2d, w_out, seq):
    n, d = x2d.shape
    tm = ROW_TILE
    seq_tiles = seq // tm
    width = N_ATTN_GROUPS * GROUP_WIDTH
    dils = [dil for _, dil in DILATED_PATTERNS]
    pm = lambda i: (i // seq_tiles, i % seq_tiles, 0)
    return pl.pallas_call(
        _attn_out_kernel,
        out_shape=jax.ShapeDtypeStruct((n, d), F32),
        grid=(n // tm,),
        in_specs=[pl.BlockSpec((None, tm // dil, dil * GROUP_WIDTH), pm) for dil in dils]
                 + [pl.BlockSpec((None, tm // dil, dil * LANES), pm) for dil in dils]
                 + [pl.BlockSpec((tm, d), lambda i: (i, 0)),
                    pl.BlockSpec((width, d), lambda i: (0, 0))],
        out_specs=pl.BlockSpec((tm, d), lambda i: (i, 0)),
        scratch_shapes=[pltpu.VMEM((tm, width), BF16),
                        pltpu.VMEM((N_ATTN_GROUPS * HEADS_PER_GROUP, tm, HEAD_DIM), F32),
                        pltpu.VMEM((N_ATTN_GROUPS, tm, LANES), F32)],
        compiler_params=_params("parallel"),
        name="attn_out_project",
    )(*os_, *lses, x2d, w_out)


U32 = jnp.uint32
HIGH_HALF = np.uint32(0xFFFF0000)


def _pack_halves(x):
    c = x.shape[1] // 2
    lo = lax.bitcast_convert_type(x[:, :c].astype(BF16).astype(F32), U32)
    hi = lax.bitcast_convert_type(x[:, c:].astype(BF16).astype(F32), U32)
    return (lo >> 16) | (hi & HIGH_HALF)


def _unpack_halves(u):
    lo = lax.bitcast_convert_type(u << 16, F32)
    hi = lax.bitcast_convert_type(u & HIGH_HALF, F32)
    return jnp.concatenate([lo, hi], axis=1)


def _router_kernel(x_ref, g_ref, w_ref, b_ref, sel_ref, hp_ref, cnt_ref):
    hf = _rms(x_ref[...], g_ref[...])
    hp_ref[...] = _pack_halves(hf)
    h = hf.astype(BF16)
    logits = jnp.dot(h, w_ref[...], preferred_element_type=F32) + b_ref[...]
    lane = lax.broadcasted_iota(jnp.int32, logits.shape, 1)
    lanef = lane.astype(F32)
    big = float(LANES)
    is_grp = lane < N_EXPERT_GROUPS
    coarse = jnp.where(is_grp, logits, -jnp.inf)
    cmax = jnp.max(coarse, axis=-1, keepdims=True)
    g_sel = jnp.min(jnp.where(coarse == cmax, lanef, big), axis=-1, keepdims=True)
    den = jnp.sum(jnp.where(is_grp, jnp.exp(logits - cmax), 0.0), axis=-1, keepdims=True)
    g_gate = 1.0 / den
    lo = N_EXPERT_GROUPS + g_sel * EXPERTS_PER_GROUP
    in_grp = (lanef >= lo) & (lanef < lo + EXPERTS_PER_GROUP)
    fine = jnp.where(in_grp, logits, -jnp.inf)
    v1 = jnp.max(fine, axis=-1, keepdims=True)
    i1 = jnp.min(jnp.where(fine == v1, lanef, big), axis=-1, keepdims=True)
    fine2 = jnp.where(lanef == i1, -jnp.inf, fine)
    v2 = jnp.max(fine2, axis=-1, keepdims=True)
    i2 = jnp.min(jnp.where(fine2 == v2, lanef, big), axis=-1, keepdims=True)
    e2 = jnp.exp(v2 - v1)
    w1 = g_gate * (1.0 / (1.0 + e2))
    w2 = g_gate * (e2 / (1.0 + e2))
    sel = jnp.where(lane == 0, w1, 0.0)
    sel = jnp.where(lane == 1, w2, sel)
    sel = jnp.where(lane == 2, i1 - N_EXPERT_GROUPS, sel)
    sel = jnp.where(lane == 3, i2 - N_EXPERT_GROUPS, sel)
    sel_ref[...] = sel
    chosen = (lanef == i1) | (lanef == i2)
    cnt_ref[...] = jnp.sum(jnp.where(chosen, 1.0, 0.0), axis=0, keepdims=True)


def _route(x2d, g, w_r, b_r):
    n, d = x2d.shape
    tm = ROW_TILE
    return pl.pallas_call(
        _router_kernel,
        out_shape=(jax.ShapeDtypeStruct((n, LANES), F32), jax.ShapeDtypeStruct((n, d // 2), U32),
                   jax.ShapeDtypeStruct((n // tm, 1, LANES), F32)),
        grid=(n // tm,),
        in_specs=[pl.BlockSpec((tm, d), lambda i: (i, 0)),
                  pl.BlockSpec((1, d), lambda i: (0, 0)),
                  pl.BlockSpec((d, LANES), lambda i: (0, 0)),
                  pl.BlockSpec((1, LANES), lambda i: (0, 0))],
        out_specs=(pl.BlockSpec((tm, LANES), lambda i: (i, 0)), pl.BlockSpec((tm, d // 2), lambda i: (i, 0)),
                   pl.BlockSpec((None, 1, LANES), lambda i: (i, 0, 0))),
        compiler_params=_params("parallel"),
        name="moe_router",
    )(x2d, g, w_r, b_r)


def _expert_kernel(be_ref, first_ref, ws_ref, ne_ref, nu_ref,
                   src_ref, dst_ref,
                   h_hbm, wg_hbm, wu_hbm, wd_hbm,
                   buf_hbm,
                   xbuf, ybuf, wgb, wub, wdb, gsem, ssem, wsem, *, layer):
    i = pl.program_id(0)
    n_used = nu_ref[0]
    slot = lax.rem(i, SCATTER_SLOTS)
    prev_slot = lax.rem(i + SCATTER_SLOTS - 1, SCATTER_SLOTS)
    gslot = lax.rem(i, GATHER_SLOTS)
    rows = MOE_BLOCK

    last_block = src_ref.shape[0] - 1

    def start_gather(block, s):
        block = jnp.minimum(block, last_block)
        for r in range(rows):
            pltpu.make_async_copy(h_hbm.at[pl.ds(src_ref[block, r], 1)], xbuf.at[s, pl.ds(r, 1)],
                                  gsem.at[s]).start(priority=r % 2)

    def wait_gather(s):
        pltpu.make_async_copy(h_hbm.at[pl.ds(0, rows)], xbuf.at[s], gsem.at[s]).wait()

    def start_scatter(s):
        for r in range(rows):
            pltpu.make_async_copy(ybuf.at[s, pl.ds(r, 1)], buf_hbm.at[pl.ds(dst_ref[i, r], 1)],
                                  ssem.at[s]).start(priority=r % 2)

    def wait_scatter(s):
        pltpu.make_async_copy(ybuf.at[s], buf_hbm.at[pl.ds(0, rows)], ssem.at[s]).wait()

    def weight_copies(e, s):
        return [(pltpu.make_async_copy(hbm.at[layer, e], vmem.at[s], wsem.at[s]), 1)
                for hbm, vmem in ((wg_hbm, wgb), (wu_hbm, wub), (wd_hbm, wdb))]

    @pl.when(i == 0)
    def _():
        ybuf[...] = jnp.zeros_like(ybuf)
        for b in range(GATHER_AHEAD):
            start_gather(b, b)
        for c, q in weight_copies(be_ref[0], 0):
            c.start(priority=q)

    @pl.when(i < n_used)
    def _():
        wait_gather(gslot)

        @pl.when(i >= SCATTER_SLOTS - 1)
        def _():
            wait_scatter(slot)

        @pl.when(first_ref[i] == 1)
        def _():
            s = ws_ref[i]
            for c, _ in weight_copies(be_ref[i], s):
                c.wait()

            @pl.when(ne_ref[i] >= 0)
            def _():
                for c, q in weight_copies(ne_ref[i], 1 - s):
                    c.start(priority=q)

        start_scatter(prev_slot)
        ws = ws_ref[i]
        h = _unpack_halves(xbuf[gslot]).astype(BF16)
        gate = jnp.dot(h, wgb[ws].astype(BF16), preferred_element_type=F32)
        up = jnp.dot(h, wub[ws].astype(BF16), preferred_element_type=F32)
        act = (gate * (1.0 / (1.0 + jnp.exp(-gate))) * up).astype(BF16)
        ybuf[slot] = _pack_halves(jnp.dot(act, wdb[ws].astype(BF16), preferred_element_type=F32))
        start_gather(i + GATHER_AHEAD, lax.rem(i + GATHER_AHEAD, GATHER_SLOTS))

    @pl.when(i == n_used)
    def _():
        start_scatter(prev_slot)
        for b in range(SCATTER_SLOTS):
            @pl.when(i >= b)
            def _(b=b):
                wait_scatter(lax.rem(i + 2 * SCATTER_SLOTS - 1 - b, SCATTER_SLOTS))
        for b in range(GATHER_AHEAD):
            wait_gather(lax.rem(i + b, GATHER_SLOTS))


def _expert_mlp(hp, w_gate, w_up, w_down, layer, tables, slot_src, slot_dst):
    n, dp = hp.shape
    d_exp = w_gate.shape[-1]
    d = w_gate.shape[-2]
    rows = MOE_BLOCK
    n_blocks = slot_src.shape[0]
    grid_spec = pltpu.PrefetchScalarGridSpec(
        num_scalar_prefetch=len(tables) + 2,
        grid=(n_blocks + 1,),
        in_specs=[
            pl.BlockSpec(memory_space=pl.ANY),
            pl.BlockSpec(memory_space=pl.ANY),
            pl.BlockSpec(memory_space=pl.ANY),
            pl.BlockSpec(memory_space=pl.ANY),
        ],
        out_specs=pl.BlockSpec(memory_space=pl.ANY),
        scratch_shapes=[pltpu.VMEM((GATHER_SLOTS, rows, dp), U32), pltpu.VMEM((SCATTER_SLOTS, rows, dp), U32),
                        pltpu.VMEM((2, d, d_exp), F32), pltpu.VMEM((2, d, d_exp), F32),
                        pltpu.VMEM((2, d_exp, d), F32),
                        pltpu.SemaphoreType.DMA((GATHER_SLOTS,)), pltpu.SemaphoreType.DMA((SCATTER_SLOTS,)),
                        pltpu.SemaphoreType.DMA((2,))],
    )
    return pl.pallas_call(
        functools.partial(_expert_kernel, layer=layer),
        out_shape=jax.ShapeDtypeStruct((TOP_K * n + rows, dp), U32),
        grid_spec=grid_spec,
        compiler_params=_params("arbitrary"),
        name="moe_expert_mlp",
    )(*tables, slot_src, slot_dst, hp, w_gate, w_up, w_down)


def _dispatch_plan(sel, tile_counts, n):
    n_assign = n * TOP_K
    n_blocks = -(-n_assign // MOE_BLOCK) + N_EXPERTS
    e_flat = sel[:, TOP_K:2 * TOP_K].astype(jnp.int32).reshape(-1)
    experts = jnp.arange(N_EXPERTS, dtype=jnp.int32)
    lo = N_EXPERT_GROUPS
    counts = jnp.sum(tile_counts[:, 0, lo:lo + N_EXPERTS], axis=0).astype(jnp.int32)
    padded = ((counts + MOE_BLOCK - 1) // MOE_BLOCK) * MOE_BLOCK
    pend = jnp.cumsum(padded)
    pstart = pend - padded
    start = jnp.cumsum(counts) - counts
    order = jnp.argsort(e_flat, stable=True).astype(jnp.int32)
    n_used = pend[-1] // MOE_BLOCK
    blk = jnp.arange(n_blocks, dtype=jnp.int32)
    used = blk < n_used

    def per_block(onehot, table):
        return jnp.sum(jnp.where(onehot, table[None, :], 0), axis=1)

    block_e = jnp.minimum(jnp.sum((pend[None, :] <= (blk * MOE_BLOCK)[:, None]).astype(jnp.int32), axis=1),
                          N_EXPERTS - 1)
    last_e = jnp.sum(jnp.where(blk == n_used - 1, block_e, 0))
    block_e = jnp.where(used, block_e, last_e)
    onehot = block_e[:, None] == experts[None, :]
    first_row = blk * MOE_BLOCK - per_block(onehot, pstart)
    n_valid = jnp.where(used, jnp.clip(per_block(onehot, counts) - first_row, 0, MOE_BLOCK), 0)
    nonempty = counts > 0
    w_slot = lax.rem(jnp.cumsum(nonempty.astype(jnp.int32)) - 1, 2)
    later = nonempty[None, :] & (experts[None, :] > experts[:, None])
    next_e = jnp.min(jnp.where(later, experts[None, :], N_EXPERTS), axis=1)
    next_e = jnp.where(next_e == N_EXPERTS, -1, next_e)
    i32 = lambda v: v.astype(jnp.int32)
    pad1 = lambda v, fill: i32(jnp.concatenate([v, jnp.full((1,), fill, v.dtype)]))
    tables = (pad1(block_e, 0), pad1(i32(used & (first_row == 0)), 0),
              pad1(per_block(onehot, w_slot), 0), pad1(per_block(onehot, next_e), -1),
              i32(n_used.reshape(1)))
    in_blk = jnp.arange(MOE_BLOCK, dtype=jnp.int32)[None, :]
    sorted_pos = (per_block(onehot, start) + first_row)[:, None] + in_blk
    a = order[jnp.clip(sorted_pos, 0, n_assign - 1)]
    valid = in_blk < n_valid[:, None]
    tok = a // TOP_K
    tail = jnp.broadcast_to(TOP_K * n + in_blk, a.shape)
    slot_src = i32(jnp.where(valid, tok, 0))
    slot_dst = jnp.where(valid, (a % TOP_K) * n + tok, tail)
    slot_dst = i32(jnp.concatenate([tail[:1], slot_dst]))
    return tables, slot_src, slot_dst


def _combined(x_ref, sel_ref, y0_ref, y1_ref):
    sel = sel_ref[...]
    y0, y1 = _unpack_halves(y0_ref[...]), _unpack_halves(y1_ref[...])
    return x_ref[...] + (y0 * sel[:, 0:1] + y1 * sel[:, 1:2])


def _pending_specs(tm, n, d):
    tiles = n // tm
    return [pl.BlockSpec((tm, d), lambda i: (i, 0)),
            pl.BlockSpec((tm, LANES), lambda i: (i, 0)),
            pl.BlockSpec((tm, d // 2), lambda i: (i, 0)),
            pl.BlockSpec((tm, d // 2), lambda i: (i + tiles, 0))]


def _combine_kernel(x_ref, sel_ref, y0_ref, y1_ref, o_ref):
    o_ref[...] = _combined(x_ref, sel_ref, y0_ref, y1_ref)


def _combine(pending):
    x2d, sel, buf = pending
    n, d = x2d.shape
    tm = ROW_TILE
    return pl.pallas_call(
        _combine_kernel,
        out_shape=jax.ShapeDtypeStruct((n, d), F32),
        grid=(n // tm,),
        in_specs=_pending_specs(tm, n, d),
        out_specs=pl.BlockSpec((tm, d), lambda i: (i, 0)),
        compiler_params=_params("parallel"),
        name="moe_combine",
    )(x2d, sel, buf, buf)


def _moe(x2d, g, w_rg, b_rg, w_re, b_re, w_gate, w_up, w_down, layer):
    n, d = x2d.shape
    pad = LANES - N_EXPERT_GROUPS - N_EXPERTS
    w_r = jnp.concatenate([w_rg, w_re, jnp.zeros((d, pad), F32)], axis=1).astype(BF16)
    b_r = jnp.concatenate([b_rg, b_re, jnp.zeros((pad,), F32)]).reshape(1, LANES)
    sel, hp, tile_counts = _route(x2d, g.reshape(1, d), w_r, b_r)
    tables, slot_src, slot_dst = _dispatch_plan(sel, tile_counts, n)
    buf = _expert_mlp(hp, w_gate, w_up, w_down, layer, tables, slot_src, slot_dst)
    return x2d, sel, buf


def _fourier_in_kernel(x_ref, sel_ref, y0_ref, y1_ref, g_ref, w_ref, ccs_ref, nyw_ref,
                       x_out_ref, a_ref, b_ref, ny_ref):
    x = _combined(x_ref, sel_ref, y0_ref, y1_ref)
    x_out_ref[...] = x
    h = _rms(x, g_ref[...]).astype(BF16)
    u = jnp.dot(h, w_ref[...], preferred_element_type=F32).astype(BF16)
    gd = ccs_ref.shape[0]
    half = gd // 2
    for gi in range(FOURIER_GROUPS):
        ab = jnp.dot(u[:, gi * gd:(gi + 1) * gd], ccs_ref[...], preferred_element_type=F32).astype(BF16)
        a_ref[:, gi * half:(gi + 1) * half] = ab[:, :half]
        b_ref[:, gi * half:(gi + 1) * half] = ab[:, half:]
    ny_ref[...] = jnp.dot(u, nyw_ref[...], preferred_element_type=F32).astype(BF16)


def _fourier_in(pending, g, w_in, ccs, nyw):
    x2d, sel, buf = pending
    n, d = x2d.shape
    tm = ROW_TILE // 2
    gd = d // FOURIER_GROUPS
    half = jax.ShapeDtypeStruct((n, d // 2), BF16)
    return pl.pallas_call(
        _fourier_in_kernel,
        out_shape=(jax.ShapeDtypeStruct((n, d), F32), half, half, jax.ShapeDtypeStruct((n, LANES), BF16)),
        grid=(n // tm,),
        in_specs=_pending_specs(tm, n, d)
                 + [pl.BlockSpec((1, d), lambda i: (0, 0)),
                    pl.BlockSpec((d, d), lambda i: (0, 0)),
                    pl.BlockSpec((gd, gd), lambda i: (0, 0)),
                    pl.BlockSpec((d, LANES), lambda i: (0, 0))],
        out_specs=(pl.BlockSpec((tm, d), lambda i: (i, 0)),
                   pl.BlockSpec((tm, d // 2), lambda i: (i, 0)),
                   pl.BlockSpec((tm, d // 2), lambda i: (i, 0)),
                   pl.BlockSpec((tm, LANES), lambda i: (i, 0))),
        compiler_params=_params("parallel"),
        name="fourier_in",
    )(x2d, sel, buf, buf, g, w_in, ccs, nyw)


def _seq_dft_kernel(cs_ref, nss_ref, a_ref, b_ref, ny_ref, fd_ref, fs_ref, fn_ref):
    p = jnp.dot(cs_ref[...], a_ref[...], preferred_element_type=F32)
    qn = jnp.dot(nss_ref[...], b_ref[...], preferred_element_type=F32)
    fd_ref[...] = (p + qn).astype(BF16)
    fs_ref[...] = (p - qn).astype(BF16)

    @pl.when(pl.program_id(2) == 0)
    def _():
        fn_ref[...] = jnp.dot(cs_ref[...], ny_ref[...], preferred_element_type=F32).astype(BF16)


def _seq_dft(cs, nss, a3, b3, ny3):
    batch, seq, dh = a3.shape
    tm = min(seq, 1024)
    tn = min(dh, 512)
    half = jax.ShapeDtypeStruct((batch, seq, dh), BF16)
    col = pl.BlockSpec((None, seq, tn), lambda b, i, j: (b, 0, j))
    out = pl.BlockSpec((None, tm, tn), lambda b, i, j: (b, i, j))
    return pl.pallas_call(
        _seq_dft_kernel,
        out_shape=(half, half, jax.ShapeDtypeStruct((batch, seq, LANES), BF16)),
        grid=(batch, seq // tm, dh // tn),
        in_specs=[pl.BlockSpec((tm, seq), lambda b, i, j: (i, 0)),
                  pl.BlockSpec((tm, seq), lambda b, i, j: (i, 0)),
                  col, col,
                  pl.BlockSpec((None, seq, LANES), lambda b, i, j: (b, 0, 0))],
        out_specs=(out, out, pl.BlockSpec((None, tm, LANES), lambda b, i, j: (b, i, 0))),
        compiler_params=_params("parallel", "parallel", "arbitrary"),
        name="seq_dft",
    )(cs, nss, a3, b3, ny3)


def _proj_residual_kernel(*refs):
    m = (len(refs) - 2) // 2
    x_ref, o_ref = refs[2 * m], refs[2 * m + 1]
    acc = x_ref[...]
    for f_ref, w_ref in zip(refs[:m], refs[m:2 * m]):
        acc = acc + jnp.dot(f_ref[...], w_ref[...], preferred_element_type=F32)
    o_ref[...] = acc


def _proj_residual(fs, ws, x2d):
    n, d = x2d.shape
    tm = ROW_TILE
    return pl.pallas_call(
        _proj_residual_kernel,
        out_shape=jax.ShapeDtypeStruct((n, d), F32),
        grid=(n // tm,),
        in_specs=[pl.BlockSpec((tm, f.shape[1]), lambda i: (i, 0)) for f in fs]
                 + [pl.BlockSpec(w.shape, lambda i: (0, 0)) for w in ws]
                 + [pl.BlockSpec((tm, d), lambda i: (i, 0))],
        out_specs=pl.BlockSpec((tm, d), lambda i: (i, 0)),
        compiler_params=_params("parallel"),
        name="proj_residual",
    )(*fs, *ws, x2d)


def _rope_tables(seq):
    half = ROT_DIM // 2
    inv_freq = ROPE_THETA ** (-np.arange(0, ROT_DIM, 2, dtype=np.float64) / ROT_DIM)
    ang = np.arange(seq, dtype=np.float64)[:, None] * inv_freq[None, :]
    cos_t = np.ones((seq, LANES))
    sa = np.zeros((seq, LANES))
    sb = np.zeros((seq, LANES))
    cos_t[:, :half] = np.cos(ang)
    cos_t[:, half:ROT_DIM] = np.cos(ang)
    sa[:, half:ROT_DIM] = np.sin(ang)
    sb[:, :half] = -np.sin(ang)
    return [jnp.asarray(t, F32) for t in (cos_t, sa, sb)]


def _dft_tables(n):
    jk = (np.arange(n, dtype=np.int64)[:, None] * np.arange(n, dtype=np.int64)[None, :]) % n
    ang = 2.0 * np.pi * jk.astype(np.float64) / n
    scale = n ** -0.5
    return np.cos(ang) * scale, np.sin(ang) * scale


def _attention_mixer(x2d, batch, seq, norm_g, w_qkv, q_g, k_g, w_out):
    n, d = x2d.shape
    width = N_ATTN_GROUPS * GROUP_WIDTH
    gw = GROUP_WIDTH
    w3 = jnp.stack([jnp.concatenate([w_qkv[:, t * width + gi * gw:t * width + (gi + 1) * gw]
                                     for t in range(3)], axis=1)
                    for gi in range(N_ATTN_GROUPS)]).astype(BF16)
    qg = (q_g * HEAD_DIM ** -0.5).reshape(1, HEAD_DIM)
    kg = k_g.reshape(1, HEAD_DIM)
    cos_t, sa_t, sb_t = _rope_tables(seq)
    qkv = _qkv_project(x2d, norm_g.reshape(1, d), w3, qg, kg, cos_t, sa_t, sb_t, batch, seq)
    outs = [_attention_group(qkv[gi], gi, batch, seq) for gi in range(N_ATTN_GROUPS)]
    return _attn_out_project([o for o, _ in outs], [l for _, l in outs], x2d, w_out.astype(BF16), seq)


def _fourier_mixer(pending, batch, seq, norm_g, w_in, w_out):
    n, d = pending[0].shape
    groups = FOURIER_GROUPS
    gd = d // groups
    half = gd // 2
    cc, sc = _dft_tables(gd)
    cs, ss = _dft_tables(seq)
    ccs = np.concatenate([cc[:, :half], sc[:, :half]], axis=1)
    nyw = np.zeros((d, LANES))
    for gi in range(groups):
        nyw[gi * gd:(gi + 1) * gd, gi] = cc[:, half]
    to_bf16 = lambda t: jnp.asarray(t, F32).astype(BF16)
    x2d, a, b, ny = _fourier_in(pending, norm_g.reshape(1, d), w_in.astype(BF16), to_bf16(ccs), to_bf16(nyw))
    fd, fs, fn = _seq_dft(to_bf16(cs), to_bf16(-ss), a.reshape(batch, seq, d // 2), b.reshape(batch, seq, d // 2),
                          ny.reshape(batch, seq, LANES))
    k = np.arange(half)
    base = (np.arange(groups) * gd)[:, None]
    rows_s = (base + (gd - k[None, :]) % gd).reshape(-1)
    keep_s = np.tile(k > 0, groups)
    w3 = w_out.reshape(groups, gd, d)
    w_d = w3[:, :half].reshape(groups * half, d).astype(BF16)
    w_s = jnp.where(jnp.asarray(keep_s)[:, None], w_out[rows_s], 0.0).astype(BF16)
    w_n = jnp.concatenate([w3[:, half], jnp.zeros((LANES - groups, d), w_out.dtype)]).astype(BF16)
    flat = lambda t: t.reshape(n, t.shape[-1])
    return _proj_residual([flat(fd), flat(fs), flat(fn)], [w_d, w_s, w_n], x2d)


def kernel(x, attn_norm_g, w_qkv, q_norm_g, k_norm_g, w_attn_out, fourier_norm_g, w_fourier_in, w_fourier_out, moe_norm_g, w_router_group, b_router_group, w_router_expert, b_router_expert, w_expert_gate, w_expert_up, w_expert_down):
    batch, seq, d = x.shape
    depth = moe_norm_g.shape[0]
    h = x.reshape(batch * seq, d)
    pending = None
    for i in range(depth):
        j = i // 2
        if i % 2 == 0:
            if pending is not None:
                h = _combine(pending)
            h = _attention_mixer(h, batch, seq, attn_norm_g[j], w_qkv[j], q_norm_g[j], k_norm_g[j], w_attn_out[j])
        else:
            h = _fourier_mixer(pending, batch, seq, fourier_norm_g[j], w_fourier_in[j], w_fourier_out[j])
        pending = _moe(h, moe_norm_g[i], w_router_group[i], b_router_group[i], w_router_expert[i],
                       b_router_expert[i], w_expert_gate, w_expert_up, w_expert_down, i)
    return _combine(pending).reshape(batch, seq, d)
```
